```python
import jax, jax.numpy as jnp
from jax import lax
import numpy as np

D_MODEL = 1024
BATCH = 8
SEQ = 4096
DEPTH = 1
DEC_BATCH = 128
DEC_SEQ = 1
PAST_LEN = 16384
PAGE_SIZE = 128

HEAD_DIM = 64
MIX_WIDTH = D_MODEL
A_HEADS = MIX_WIDTH // 2 // HEAD_DIM
B_HEADS = MIX_WIDTH // 2 // HEAD_DIM
B_KV_HEADS = 2
B_GROUP = B_HEADS // B_KV_HEADS
DILATED_CONFIGS = ((128, 1), (512, 4), (2048, 16))
BAND = 128
A_WINDOW = 2048
B_WINDOW = 128
ROT_DIM = HEAD_DIM // 4
ROPE_THETA = 500000.0
ALPHA = (2 * DEPTH) ** 0.25
BETA = (8 * DEPTH) ** -0.25
LN_EPS = 1e-5
A_W = A_HEADS * HEAD_DIM
B_W = B_HEADS * HEAD_DIM
B_KV_W = B_KV_HEADS * HEAD_DIM
IN_SPLITS = (A_W, A_W, A_W, A_W, B_W, B_KV_W, B_KV_W, B_W)
IN_WIDTH = sum(IN_SPLITS)
SPLIT_IDX = tuple(int(c) for c in np.cumsum(IN_SPLITS)[:-1])

kernel_name = "hymba_dilated_swa_sink_deepnorm_step"


def _layer_norm(x, g, b):
    xf = x.astype(jnp.float32)
    mu = xf.mean(-1, keepdims=True)
    var = jnp.square(xf - mu).mean(-1, keepdims=True)
    return ((xf - mu) * lax.rsqrt(var + LN_EPS) * g.astype(jnp.float32) + b.astype(jnp.float32)).astype(x.dtype)


def _rope(x, pos):
    inv = ROPE_THETA ** (-jnp.arange(0, ROT_DIM, 2, dtype=jnp.float32) / ROT_DIM)
    ang = pos.astype(jnp.float32)[:, None] * inv[None, :]
    cos = jnp.cos(ang)[:, None, :]
    sin = jnp.sin(ang)[:, None, :]
    xf = x.astype(jnp.float32)
    x1 = xf[..., :ROT_DIM // 2]
    x2 = xf[..., ROT_DIM // 2:ROT_DIM]
    out = jnp.concatenate([x1 * cos - x2 * sin, x2 * cos + x1 * sin, xf[..., ROT_DIM:]], axis=-1)
    return out.astype(x.dtype)


def _project(x, pos, w_in):
    b, s, _ = x.shape
    h = jnp.einsum('bsd,de->bse', x, w_in)
    qa, ka, va, ga, qb, kb, vb, gb = jnp.split(h, SPLIT_IDX, axis=-1)
    qa = _rope(qa.reshape(b, s, A_HEADS, HEAD_DIM), pos)
    ka = _rope(ka.reshape(b, s, A_HEADS, HEAD_DIM), pos)
    va = va.reshape(b, s, A_HEADS, HEAD_DIM)
    qb = _rope(qb.reshape(b, s, B_HEADS, HEAD_DIM), pos)
    kb = _rope(kb.reshape(b, s, B_KV_HEADS, HEAD_DIM), pos)
    vb = vb.reshape(b, s, B_KV_HEADS, HEAD_DIM)
    return qa, ka, va, ga, qb, kb, vb, gb


def _dilate(x, d):
    b, s = x.shape[:2]
    rest = x.shape[2:]
    L = s // d
    xs = jnp.moveaxis(x.reshape(b, L, d, *rest), 2, 1).reshape(b * d, L, *rest)
    lp = -(-L // BAND) * BAND
    return jnp.pad(xs, [(0, 0), (0, lp - L)] + [(0, 0)] * len(rest))


def _undilate(y, b, s, d):
    L = s // d
    rest = y.shape[2:]
    y = y[:, :L].reshape(b, d, L, *rest)
    return jnp.moveaxis(y, 1, 2).reshape(b, s, *rest)


def _band_partial(q, k, v):
    n, L, hk, g, dh = q.shape
    nb = L // BAND
    qb = q.reshape(n, nb, BAND, hk, g, dh).astype(jnp.float32)
    kb = k.reshape(n, nb, BAND, hk, dh).astype(jnp.float32)
    vb = v.reshape(n, nb, BAND, hk, dh).astype(jnp.float32)
    kprev = jnp.pad(kb, ((0, 0), (1, 0), (0, 0), (0, 0), (0, 0)))[:, :-1]
    vprev = jnp.pad(vb, ((0, 0), (1, 0), (0, 0), (0, 0), (0, 0)))[:, :-1]
    kk = jnp.concatenate([kprev, kb], axis=2)
    vv = jnp.concatenate([vprev, vb], axis=2)
    sc = jnp.einsum('nbqhgd,nbkhd->nbhgqk', qb, kk) * (HEAD_DIM ** -0.5)
    qi = jnp.arange(BAND)[:, None]
    ki = jnp.arange(2 * BAND)[None, :]
    dist = BAND + qi - ki
    kpos = jnp.arange(nb)[:, None, None] * BAND - BAND + ki[None]
    valid = (dist >= 0) & (dist <= BAND) & (kpos >= 0)
    sc = jnp.where(valid[None, :, None, None], sc, -jnp.inf)
    m = sc.max(-1)
    p = jnp.exp(sc - m[..., None])
    den = p.sum(-1)
    o = jnp.einsum('nbhgqk,nbkhd->nbqhgd', p, vv).reshape(n, L, hk, g, dh)
    m = jnp.transpose(m, (0, 1, 4, 2, 3)).reshape(n, L, hk, g)
    den = jnp.transpose(den, (0, 1, 4, 2, 3)).reshape(n, L, hk, g)
    return m, den, o


def _strided_partial(q, kv_all, buf_len, dilation):
    t = q.shape[1]
    j = jnp.arange(BAND + 1)
    idx = buf_len + jnp.arange(t)[:, None] - dilation * j[None, :]
    valid = idx >= 0
    kv = kv_all[:, jnp.clip(idx, 0)].astype(jnp.float32)
    k = kv[:, :, :, 0]
    v = kv[:, :, :, 1]
    sc = jnp.einsum('nqhgd,nqkhd->nqhgk', q.astype(jnp.float32), k) * (HEAD_DIM ** -0.5)
    sc = jnp.where(valid[None, :, None, None, :], sc, -jnp.inf)
    m = sc.max(-1)
    p = jnp.exp(sc - m[..., None])
    den = p.sum(-1)
    o = jnp.einsum('nqhgk,nqkhd->nqhgd', p, v)
    return m, den, o


def _merge(parts, sink=None):
    m = parts[0][0]
    for pm, _, _ in parts[1:]:
        m = jnp.maximum(m, pm)
    if sink is not None:
        m = jnp.maximum(m, sink)
    den = 0.0
    num = 0.0
    for pm, ps, po in parts:
        w = jnp.exp(pm - m)
        den = den + ps * w
        num = num + po * w[..., None]
    if sink is not None:
        den = den + jnp.exp(sink - m)
    return num / den[..., None]


def _output(x, oa, ob, ga, gb, w_o, ln_g, ln_b):
    b, s, _ = x.shape
    mix = jnp.concatenate([oa.reshape(b, s, A_W).astype(x.dtype) * jax.nn.silu(ga),
                           ob.reshape(b, s, B_W).astype(x.dtype) * jax.nn.silu(gb)], axis=-1)
    out = jnp.einsum('bse,ed->bsd', mix, w_o)
    return _layer_norm(ALPHA * x + out, ln_g, ln_b)


def _prompt_layer(x, w_in, sinks, w_o, ln_g, ln_b):
    b, s, _ = x.shape
    pos = jnp.arange(s)
    qa, ka, va, ga, qb, kb, vb, gb = _project(x, pos, w_in)
    parts = []
    for _, d in DILATED_CONFIGS:
        m, den, o = _band_partial(_dilate(qa[:, :, :, None], d), _dilate(ka, d), _dilate(va, d))
        parts.append((_undilate(m, b, s, d), _undilate(den, b, s, d), _undilate(o, b, s, d)))
    oa = _merge(parts)
    qbg = qb.reshape(b, s, B_KV_HEADS, B_GROUP, HEAD_DIM)
    m, den, o = _band_partial(_dilate(qbg, 1), _dilate(kb, 1), _dilate(vb, 1))
    sink = sinks.astype(jnp.float32).reshape(B_KV_HEADS, B_GROUP)
    ob = _merge([(_undilate(m, b, s, 1), _undilate(den, b, s, 1), _undilate(o, b, s, 1))], sink)
    y = _output(x, oa, ob, ga, gb, w_o, ln_g, ln_b)
    wa = min(A_WINDOW, s)
    wb = min(B_WINDOW, s)
    kv_a = jnp.stack([ka[:, s - wa:], va[:, s - wa:]], axis=2)
    kv_b = jnp.stack([kb[:, s - wb:], vb[:, s - wb:]], axis=2)
    return y, kv_a, kv_b


def _sample_layer(x, kv_a_cache, kv_b_cache, w_in, sinks, w_o, ln_g, ln_b):
    b, t, _ = x.shape
    pos = PAST_LEN + jnp.arange(t)
    qa, ka, va, ga, qb, kb, vb, gb = _project(x, pos, w_in)
    new_a = jnp.stack([ka, va], axis=2)
    new_b = jnp.stack([kb, vb], axis=2)
    all_a = jnp.concatenate([kv_a_cache.astype(new_a.dtype), new_a], axis=1)
    all_b = jnp.concatenate([kv_b_cache.astype(new_b.dtype), new_b], axis=1)
    buf_a = kv_a_cache.shape[1]
    buf_b = kv_b_cache.shape[1]
    parts = [_strided_partial(qa[:, :, :, None], all_a, buf_a, d) for _, d in DILATED_CONFIGS]
    oa = _merge(parts)
    qbg = qb.reshape(b, t, B_KV_HEADS, B_GROUP, HEAD_DIM)
    sink = sinks.astype(jnp.float32).reshape(B_KV_HEADS, B_GROUP)
    ob = _merge([_strided_partial(qbg, all_b, buf_b, 1)], sink)
    y = _output(x, oa, ob, ga, gb, w_o, ln_g, ln_b)
    return y, new_a, new_b


def setup_inputs(seed: int = 0) -> dict:
    key = jax.random.key(seed)
    ks = jax.random.split(key, 9)
    buf_a = min(A_WINDOW, PAST_LEN)
    buf_b = min(B_WINDOW, PAST_LEN)
    x_prompt = jax.random.normal(ks[0], (BATCH, SEQ, D_MODEL), jnp.float32)
    x_sample = jax.random.normal(ks[1], (DEC_BATCH, DEC_SEQ, D_MODEL), jnp.float32)
    cache_a_kv = jax.random.normal(ks[2], (DEPTH, DEC_BATCH, buf_a, 2, A_HEADS, HEAD_DIM), jnp.float32)
    cache_b_kv = jax.random.normal(ks[3], (DEPTH, DEC_BATCH, buf_b, 2, B_KV_HEADS, HEAD_DIM), jnp.float32)
    w_in = jax.random.normal(ks[4], (DEPTH, D_MODEL, IN_WIDTH), jnp.float32) * D_MODEL ** -0.5
    attn_sinks = 0.5 * jax.random.normal(ks[5], (DEPTH, B_HEADS), jnp.float32)
    w_o = jax.random.normal(ks[6], (DEPTH, MIX_WIDTH, D_MODEL), jnp.float32) * (MIX_WIDTH ** -0.5 * BETA)
    ln_g = 1.0 + 0.05 * jax.random.normal(ks[7], (DEPTH, D_MODEL), jnp.float32)
    ln_b = 0.05 * jax.random.normal(ks[8], (DEPTH, D_MODEL), jnp.float32)
    return {"x_prompt": x_prompt, "x_sample": x_sample, "cache_a_kv": cache_a_kv, "cache_b_kv": cache_b_kv,
            "w_in": w_in, "attn_sinks": attn_sinks, "w_o": w_o, "ln_g": ln_g, "ln_b": ln_b}


def reference(x_prompt, x_sample, cache_a_kv, cache_b_kv, w_in, attn_sinks, w_o, ln_g, ln_b):
    yp = x_prompt
    ys = x_sample
    pa, pb, sa, sb = [], [], [], []
    for l in range(DEPTH):
        yp, kva, kvb = _prompt_layer(yp, w_in[l], attn_sinks[l], w_o[l], ln_g[l], ln_b[l])
        ys, nka, nkb = _sample_layer(ys, cache_a_kv[l], cache_b_kv[l], w_in[l], attn_sinks[l], w_o[l], ln_g[l], ln_b[l])
        pa.append(kva)
        pb.append(kvb)
        sa.append(nka)
        sb.append(nkb)
    prompt_a_kv = jnp.stack(pa)
    prompt_b_kv = jnp.stack(pb)
    sample_a_kv = jnp.stack(sa)
    sample_b_kv = jnp.stack(sb)
    return (yp, ys, prompt_a_kv, prompt_b_kv, sample_a_kv, sample_b_kv)
```

```python
import functools

import jax
import jax.numpy as jnp
import numpy as np
from jax import lax
from jax.experimental import pallas as pl
from jax.experimental.pallas import tpu as pltpu

D_MODEL = 1024
HEAD_DIM = 64
A_HEADS = 8
B_HEADS = 8
B_KV_HEADS = 2
DILATIONS = (1, 4, 16)
BAND = 128
A_WINDOW = 2048
B_WINDOW = 128
PAST_LEN = 16384
ROT_DIM = HEAD_DIM // 4
ROPE_THETA = 500000.0
LN_EPS = 1e-5
A_W = A_HEADS * HEAD_DIM
B_W = B_HEADS * HEAD_DIM
B_KV_W = B_KV_HEADS * HEAD_DIM
LANES = 128
HQ_W = 3 * A_W + B_W + 2 * B_KV_W
G_W = A_W + B_W
IN_WIDTH = HQ_W + G_W
QA_T, KA_T, VA_T = 0, A_W // LANES, 2 * A_W // LANES
QB_T = 3 * A_W // LANES
KB_T = QB_T + B_W // LANES
VB_T = KB_T + 1
VMEM_LIMIT = 56 * 1024 * 1024
COL_CHUNK = 256
NEG_INF = float("-inf")


def _proj_kernel(x_ref, w_ref, cos_ref, sinp_ref, sinm_ref, hq_ref, sg_ref):
    x = x_ref[...].astype(jnp.bfloat16)
    cos = cos_ref[...]
    sinp = sinp_ref[...]
    sinm = sinm_ref[...]
    rope_tiles = set(range(QA_T, VA_T)) | set(range(QB_T, VB_T))
    for c in range(IN_WIDTH // COL_CHUNK):
        acc = jnp.dot(x, w_ref[:, c * COL_CHUNK:(c + 1) * COL_CHUNK],
                      preferred_element_type=jnp.float32)
        for half in range(COL_CHUNK // LANES):
            tile = c * (COL_CHUNK // LANES) + half
            t = acc[:, half * LANES:(half + 1) * LANES]
            if tile < HQ_W // LANES:
                if tile in rope_tiles:
                    t = (t * cos + pltpu.roll(t, ROT_DIM // 2, 1) * sinp
                         + pltpu.roll(t, LANES - ROT_DIM // 2, 1) * sinm)
                hq_ref[:, tile * LANES:(tile + 1) * LANES] = t
            else:
                g = tile - HQ_W // LANES
                sg_ref[:, g * LANES:(g + 1) * LANES] = (t * jax.nn.sigmoid(t)).astype(sg_ref.dtype)


def _project(x2, w, cos, sinp, sinm, tm):
    n = x2.shape[0]
    tab_blocks = cos.shape[0] // tm
    row = lambda i: (i, 0)
    tab = lambda i: (i % tab_blocks, 0)
    return pl.pallas_call(
        _proj_kernel,
        grid=(n // tm,),
        in_specs=[
            pl.BlockSpec((tm, D_MODEL), row),
            pl.BlockSpec((D_MODEL, IN_WIDTH), lambda i: (0, 0)),
            pl.BlockSpec((tm, LANES), tab),
            pl.BlockSpec((tm, LANES), tab),
            pl.BlockSpec((tm, LANES), tab),
        ],
        out_specs=[pl.BlockSpec((tm, HQ_W), row), pl.BlockSpec((tm, G_W), row)],
        out_shape=[jax.ShapeDtypeStruct((n, HQ_W), jnp.float32),
                   jax.ShapeDtypeStruct((n, G_W), jnp.bfloat16)],
        compiler_params=pltpu.CompilerParams(
            dimension_semantics=("arbitrary",), vmem_limit_bytes=VMEM_LIMIT),
        name="proj",
    )(x2, w, cos, sinp, sinm)


def _lane_lo(shape):
    return lax.broadcasted_iota(jnp.int32, shape, len(shape) - 1) < HEAD_DIM


def _attn_kernel(*refs, dilations, seq, with_sink):
    if with_sink:
        q_ref, k_ref, v_ref, bias_ref, sink_ref, o_ref = refs[:6]
        scratch = refs[6:]
    else:
        q_ref, k_ref, v_ref, bias_ref, o_ref = refs[:5]
        scratch = refs[5:]
    multi = len(dilations) > 1
    if multi:
        acc_ref, m_ref, l_ref = scratch
    lo = _lane_lo((BAND, LANES))
    ones = jnp.ones((2 * BAND, LANES), jnp.bfloat16)

    def pair(a, b):
        return jnp.where(lo, a, b)

    for ci, d in enumerate(dilations):
        first = ci == 0
        last = ci == len(dilations) - 1
        nblk = seq // d // BAND

        def rows(start, d=d):
            return pl.ds(start, BAND) if d == 1 else pl.ds(start, BAND, stride=d)

        def block(i, carry, r, d=d, first=first, last=last, rows=rows):
            kp, vp = carry
            start = r + i * (BAND * d)
            q = q_ref[rows(start), :]
            kc = k_ref[rows(start), :].astype(jnp.bfloat16)
            vc = v_ref[rows(start), :].astype(jnp.bfloat16)
            q2 = jnp.concatenate([jnp.where(lo, q, 0.0), jnp.where(lo, 0.0, q)],
                                 axis=0).astype(jnp.bfloat16)
            kk = jnp.concatenate([kp, kc], axis=0)
            s = lax.dot_general(q2, kk, (((1,), (1,)), ((), ())),
                                preferred_element_type=jnp.float32)
            s = s + bias_ref[jnp.minimum(i, 1)]
            m = jnp.max(s, axis=1, keepdims=True)
            p = jnp.exp(s - m).astype(jnp.bfloat16)
            vv = jnp.concatenate([jnp.concatenate([vp, vc], axis=0), ones], axis=1)
            pv = jnp.dot(p, vv, preferred_element_type=jnp.float32)
            o_n = pair(pv[:BAND, :LANES], pv[BAND:, :LANES])
            l_n = pair(pv[:BAND, LANES:], pv[BAND:, LANES:])
            m_n = pair(jnp.broadcast_to(m[:BAND], (BAND, LANES)),
                       jnp.broadcast_to(m[BAND:], (BAND, LANES)))
            if multi and not first:
                m_o = m_ref[rows(start), :]
                m_t = jnp.maximum(m_o, m_n)
                a_o = jnp.exp(m_o - m_t)
                a_n = jnp.exp(m_n - m_t)
                o_n = acc_ref[rows(start), :] * a_o + o_n * a_n
                l_n = l_ref[rows(start), :] * a_o + l_n * a_n
                m_n = m_t
            if last:
                if with_sink:
                    sink = sink_ref[...]
                    m_t = jnp.maximum(m_n, sink)
                    a_n = jnp.exp(m_n - m_t)
                    o_n = o_n * a_n
                    l_n = l_n * a_n + jnp.exp(sink - m_t)
                o_ref[rows(start), :] = (o_n / l_n).astype(o_ref.dtype)
            else:
                acc_ref[rows(start), :] = o_n
                l_ref[rows(start), :] = l_n
                m_ref[rows(start), :] = m_n
            return kc, vc

        zero = jnp.zeros((BAND, LANES), jnp.bfloat16)

        def residue(r, _, nblk=nblk, block=block):
            lax.fori_loop(0, nblk, functools.partial(block, r=r), (zero, zero))
            return 0

        if d == 1:
            residue(0, 0)
        else:
            lax.fori_loop(0, d, residue, 0)


def _band_bias():
    qi = np.arange(BAND)[:, None]
    ki = np.arange(2 * BAND)[None, :]
    dist = BAND + qi - ki
    band = (dist >= 0) & (dist <= BAND)
    full = np.where(band, 0.0, NEG_INF).astype(np.float32)
    head = np.where(band & (ki >= BAND), 0.0, NEG_INF).astype(np.float32)
    both = np.stack([head, full])
    return jnp.asarray(np.concatenate([both, both], axis=1))


def _attention(hq3, q_tile, k_tile, v_tile, shared_kv, dilations, sink=None):
    b, seq, _ = hq3.shape
    n_pairs = A_W // LANES
    with_sink = sink is not None
    kv_off = (lambda u: 0) if shared_kv else (lambda u: u)
    in_specs = [
        pl.BlockSpec((None, seq, LANES), lambda i, u: (i, 0, q_tile + u)),
        pl.BlockSpec((None, seq, LANES), lambda i, u: (i, 0, k_tile + kv_off(u))),
        pl.BlockSpec((None, seq, LANES), lambda i, u: (i, 0, v_tile + kv_off(u))),
        pl.BlockSpec((2, 2 * BAND, 2 * BAND), lambda i, u: (0, 0, 0)),
    ]
    args = [hq3, hq3, hq3, _band_bias()]
    if with_sink:
        in_specs.append(pl.BlockSpec((None, 1, LANES), lambda i, u: (u, 0, 0)))
        args.append(sink)
    scratch = []
    if len(dilations) > 1:
        scratch = [pltpu.VMEM((seq, LANES), jnp.float32)] * 3
    return pl.pallas_call(
        functools.partial(_attn_kernel, dilations=dilations, seq=seq, with_sink=with_sink),
        grid=(b, n_pairs),
        in_specs=in_specs,
        out_specs=pl.BlockSpec((None, seq, LANES), lambda i, u: (i, 0, u)),
        out_shape=jax.ShapeDtypeStruct((b, seq, n_pairs * LANES), jnp.bfloat16),
        scratch_shapes=scratch,
        compiler_params=pltpu.CompilerParams(
            dimension_semantics=("arbitrary", "arbitrary"), vmem_limit_bytes=VMEM_LIMIT),
        name="attn_b" if with_sink else "attn_a",
    )(*args)


def _sample_kernel(hq_ref, ca_ref, cb_ref, w_ref, sink_ref, oa_ref, ob_ref):
    row = hq_ref[...]
    qa = row[:, QA_T * LANES:KA_T * LANES]
    ka_new = row[:, KA_T * LANES:VA_T * LANES]
    va_new = row[:, VA_T * LANES:QB_T * LANES]
    qb = row[:, QB_T * LANES:KB_T * LANES]
    kb_new = row[:, KB_T * LANES:VB_T * LANES]
    vb_new = row[:, VB_T * LANES:HQ_W]

    hrow = lax.broadcasted_iota(jnp.int32, (A_HEADS, A_W), 0)
    hcol = lax.broadcasted_iota(jnp.int32, (A_HEADS, A_W), 1) // HEAD_DIM
    own = hrow == hcol
    q_bd = jnp.where(own, jnp.broadcast_to(qa, (A_HEADS, A_W)), 0.0)
    s = jnp.dot(q_bd.astype(jnp.bfloat16), ca_ref[0].astype(jnp.bfloat16),
                preferred_element_type=jnp.float32)
    s_new = jnp.sum(q_bd * ka_new, axis=1, keepdims=True)
    w = w_ref[...]
    s = jnp.where(w > 0.0, s, NEG_INF)
    m = jnp.maximum(jnp.max(s, axis=1, keepdims=True), s_new)
    p = jnp.exp(s - m) * w
    p_new = float(len(DILATIONS)) * jnp.exp(s_new - m)
    den = jnp.sum(p, axis=1, keepdims=True) + p_new
    o = lax.dot_general(p.astype(jnp.bfloat16), ca_ref[1].astype(jnp.bfloat16),
                        (((1,), (1,)), ((), ())), preferred_element_type=jnp.float32)
    o = (o + p_new * va_new) / den
    oa_ref[...] = jnp.sum(jnp.where(own, o, 0.0), axis=0, keepdims=True)

    prow = lax.broadcasted_iota(jnp.int32, (B_HEADS, LANES), 0)
    pcol = lax.broadcasted_iota(jnp.int32, (B_HEADS, LANES), 1) // HEAD_DIM
    qb_rows = jnp.zeros((B_HEADS, LANES), jnp.float32)
    for t in range(B_W // LANES):
        qb_rows = jnp.where(prow // 2 == t,
                            jnp.broadcast_to(qb[:, t * LANES:(t + 1) * LANES], (B_HEADS, LANES)), qb_rows)
    own_b = (prow % 2) == pcol
    qb_bd = jnp.where(own_b, qb_rows, 0.0)
    sb = jnp.dot(qb_bd.astype(jnp.bfloat16), cb_ref[0].astype(jnp.bfloat16),
                 preferred_element_type=jnp.float32)
    sb_new = jnp.sum(qb_bd * kb_new, axis=1, keepdims=True)
    sink = sink_ref[...]
    mb = jnp.maximum(jnp.maximum(jnp.max(sb, axis=1, keepdims=True), sb_new), sink)
    pb = jnp.exp(sb - mb)
    pb_new = jnp.exp(sb_new - mb)
    den_b = jnp.sum(pb, axis=1, keepdims=True) + pb_new + jnp.exp(sink - mb)
    ob = lax.dot_general(pb.astype(jnp.bfloat16), cb_ref[1].astype(jnp.bfloat16),
                         (((1,), (1,)), ((), ())), preferred_element_type=jnp.float32)
    ob = (ob + pb_new * vb_new) / den_b
    lo = _lane_lo((1, LANES))
    for t in range(B_W // LANES):
        ob_ref[:, t * LANES:(t + 1) * LANES] = jnp.where(lo, ob[2 * t:2 * t + 1], ob[2 * t + 1:2 * t + 2])


def _sample_weights(buf):
    dist = buf - np.arange(buf)
    w = np.zeros((buf,), np.float32)
    for d in DILATIONS:
        w += ((dist % d == 0) & (dist <= BAND * d)).astype(np.float32)
    return jnp.asarray(w[None, :])


def _sample_attention(hq_s, cache_a_t, cache_b_t, sink_col):
    n = hq_s.shape[0]
    buf_a = cache_a_t.shape[-1]
    buf_b = cache_b_t.shape[-1]
    return pl.pallas_call(
        _sample_kernel,
        grid=(n,),
        in_specs=[
            pl.BlockSpec((None, 1, HQ_W), lambda i: (i, 0, 0)),
            pl.BlockSpec((None, 2, A_W, buf_a), lambda i: (i, 0, 0, 0)),
            pl.BlockSpec((None, 2, B_KV_W, buf_b), lambda i: (i, 0, 0, 0)),
            pl.BlockSpec((1, buf_a), lambda i: (0, 0)),
            pl.BlockSpec((B_HEADS, 1), lambda i: (0, 0)),
        ],
        out_specs=[pl.BlockSpec((None, 1, A_W), lambda i: (i, 0, 0)),
                   pl.BlockSpec((None, 1, B_W), lambda i: (i, 0, 0))],
        out_shape=[jax.ShapeDtypeStruct((n, 1, A_W), jnp.float32),
                   jax.ShapeDtypeStruct((n, 1, B_W), jnp.float32)],
        compiler_params=pltpu.CompilerParams(
            dimension_semantics=("arbitrary",), vmem_limit_bytes=VMEM_LIMIT),
        name="sample_attn",
    )(hq_s.reshape(n, 1, HQ_W), cache_a_t, cache_b_t, _sample_weights(buf_a), sink_col)


def _out_kernel(x_ref, oa_ref, ob_ref, sg_ref, wo_ref, g_ref, b_ref, y_ref, *, alpha):
    sg = sg_ref[...].astype(jnp.float32)
    mix_a = (oa_ref[...].astype(jnp.float32) * sg[:, :A_W]).astype(jnp.bfloat16)
    mix_b = (ob_ref[...].astype(jnp.float32) * sg[:, A_W:]).astype(jnp.bfloat16)
    out = (jnp.dot(mix_a, wo_ref[:A_W, :], preferred_element_type=jnp.float32)
           + jnp.dot(mix_b, wo_ref[A_W:, :], preferred_element_type=jnp.float32))
    z = alpha * x_ref[...] + out
    mu = jnp.mean(z, axis=1, keepdims=True)
    zc = z - mu
    var = jnp.mean(zc * zc, axis=1, keepdims=True)
    y_ref[...] = zc * lax.rsqrt(var + LN_EPS) * g_ref[...] + b_ref[...]


def _output(x2, oa, ob, sg, wo, g, b, alpha, tm):
    n = x2.shape[0]
    row = lambda i: (i, 0)
    fixed = lambda i: (0, 0)
    return pl.pallas_call(
        functools.partial(_out_kernel, alpha=alpha),
        grid=(n // tm,),
        in_specs=[
            pl.BlockSpec((tm, D_MODEL), row),
            pl.BlockSpec((tm, A_W), row),
            pl.BlockSpec((tm, B_W), row),
            pl.BlockSpec((tm, G_W), row),
            pl.BlockSpec((G_W, D_MODEL), fixed),
            pl.BlockSpec((1, D_MODEL), fixed),
            pl.BlockSpec((1, D_MODEL), fixed),
        ],
        out_specs=pl.BlockSpec((tm, D_MODEL), row),
        out_shape=jax.ShapeDtypeStruct((n, D_MODEL), jnp.float32),
        compiler_params=pltpu.CompilerParams(
            dimension_semantics=("arbitrary",), vmem_limit_bytes=VMEM_LIMIT),
        name="out_proj",
    )(x2, oa, ob, sg, wo, g, b)


def _rope_tables(pos):
    inv = ROPE_THETA ** (-jnp.arange(0, ROT_DIM, 2, dtype=jnp.float32) / ROT_DIM)
    ang = pos.astype(jnp.float32)[:, None] * inv[None, :]
    cos = jnp.cos(ang)
    sin = jnp.sin(ang)
    n = pos.shape[0]
    half = ROT_DIM // 2
    pad = HEAD_DIM - ROT_DIM
    one = jnp.ones((n, pad), jnp.float32)
    zero = jnp.zeros((n, pad), jnp.float32)
    zh = jnp.zeros((n, half), jnp.float32)
    cos_h = jnp.concatenate([cos, cos, one], axis=1)
    sinp_h = jnp.concatenate([zh, sin, zero], axis=1)
    sinm_h = jnp.concatenate([-sin, zh, zero], axis=1)
    rep = LANES // HEAD_DIM
    return tuple(jnp.tile(t, (1, rep)) for t in (cos_h, sinp_h, sinm_h))


def _b_head_perm():
    group = B_HEADS // B_KV_HEADS
    cols = []
    for t in range(group):
        for e in range(B_KV_HEADS):
            g = t + group * e
            cols.extend(range(g * HEAD_DIM, (g + 1) * HEAD_DIM))
    return np.asarray(cols, np.int32)


def _layer(xp, xs, cache_a, cache_b, w_in, sinks, w_o, ln_g, ln_b, alpha, past_len):
    b, seq, _ = xp.shape
    nb = xs.shape[0]
    perm = _b_head_perm()
    o_qa, o_ka, o_va, o_ga = 0, A_W, 2 * A_W, 3 * A_W
    o_qb = 4 * A_W
    o_kb = o_qb + B_W
    o_vb = o_kb + B_KV_W
    o_gb = o_vb + B_KV_W
    ar = np.arange
    cols = np.concatenate([ar(o_qa, o_qa + A_W), ar(o_ka, o_ka + A_W), ar(o_va, o_va + A_W),
                           o_qb + perm, ar(o_kb, o_kb + B_KV_W), ar(o_vb, o_vb + B_KV_W),
                           ar(o_ga, o_ga + A_W), o_gb + perm]).astype(np.int32)
    scale = np.ones((IN_WIDTH,), np.float32)
    scale[:A_W] = HEAD_DIM ** -0.5
    scale[QB_T * LANES:KB_T * LANES] = HEAD_DIM ** -0.5
    w = (w_in[:, cols] * scale[None, :]).astype(jnp.bfloat16)
    rows_o = np.concatenate([ar(A_W), A_W + perm]).astype(np.int32)
    wo = w_o[rows_o, :].astype(jnp.bfloat16)
    sink_p = sinks[perm[::HEAD_DIM] // HEAD_DIM].astype(jnp.float32)
    sink_tiles = jnp.repeat(sink_p, HEAD_DIM).reshape(B_W // LANES, 1, LANES)
    g2 = ln_g.reshape(1, D_MODEL)
    b2 = ln_b.reshape(1, D_MODEL)

    tm = 512
    x2 = xp.reshape(b * seq, D_MODEL)
    hq, sg = _project(x2, w, *_rope_tables(jnp.arange(seq)), tm)
    hq3 = hq.reshape(b, seq, HQ_W)
    oa = _attention(hq3, QA_T, KA_T, VA_T, False, tuple(sorted(DILATIONS, reverse=True)))
    ob = _attention(hq3, QB_T, KB_T, VB_T, True, (1,), sink=sink_tiles)
    y = _output(x2, oa.reshape(b * seq, A_W), ob.reshape(b * seq, B_W), sg, wo, g2, b2, alpha, tm)
    wa = min(A_WINDOW, seq)
    wb = min(B_WINDOW, seq)
    ka = hq3[:, seq - wa:, KA_T * LANES:VA_T * LANES].reshape(b, wa, A_HEADS, HEAD_DIM)
    va = hq3[:, seq - wa:, VA_T * LANES:QB_T * LANES].reshape(b, wa, A_HEADS, HEAD_DIM)
    kb = hq3[:, seq - wb:, KB_T * LANES:VB_T * LANES].reshape(b, wb, B_KV_HEADS, HEAD_DIM)
    vb = hq3[:, seq - wb:, VB_T * LANES:HQ_W].reshape(b, wb, B_KV_HEADS, HEAD_DIM)
    kv_a = jnp.stack([ka, va], axis=2)
    kv_b = jnp.stack([kb, vb], axis=2)

    xs2 = xs.reshape(nb, D_MODEL)
    pos_s = jnp.full((nb,), past_len, jnp.int32)
    hq_s, sg_s = _project(xs2, w, *_rope_tables(pos_s), nb)
    ca_t = jnp.transpose(cache_a, (0, 2, 3, 4, 1)).reshape(nb, 2, A_W, cache_a.shape[1])
    cb_t = jnp.transpose(cache_b, (0, 2, 3, 4, 1)).reshape(nb, 2, B_KV_W, cache_b.shape[1])
    oa_s, ob_s = _sample_attention(hq_s, ca_t, cb_t, sink_p.reshape(B_HEADS, 1))
    y_s = _output(xs2, oa_s.reshape(nb, A_W), ob_s.reshape(nb, B_W), sg_s, wo, g2, b2, alpha, nb)
    new_a = jnp.stack([hq_s[:, KA_T * LANES:VA_T * LANES].reshape(nb, 1, A_HEADS, HEAD_DIM),
                       hq_s[:, VA_T * LANES:QB_T * LANES].reshape(nb, 1, A_HEADS, HEAD_DIM)], axis=2)
    new_b = jnp.stack([hq_s[:, KB_T * LANES:VB_T * LANES].reshape(nb, 1, B_KV_HEADS, HEAD_DIM),
                       hq_s[:, VB_T * LANES:HQ_W].reshape(nb, 1, B_KV_HEADS, HEAD_DIM)], axis=2)
    return (y.reshape(b, seq, D_MODEL), y_s.reshape(nb, 1, D_MODEL), kv_a, kv_b, new_a, new_b)


def kernel(x_prompt, x_sample, cache_a_kv, cache_b_kv, w_in, attn_sinks, w_o, ln_g, ln_b):
    depth = w_in.shape[0]
    assert depth == 1 and x_sample.shape[1] == 1, "single layer, one sample token per sequence"
    assert cache_a_kv.shape[2] == BAND * max(DILATIONS), "mixer-A cache covers every strided read"
    assert cache_b_kv.shape[2] == B_WINDOW
    alpha = (2 * depth) ** 0.25
    outs = _layer(x_prompt, x_sample, cache_a_kv[0], cache_b_kv[0], w_in[0], attn_sinks[0],
                  w_o[0], ln_g[0], ln_b[0], alpha, PAST_LEN)
    yp, ys, kv_a, kv_b, new_a, new_b = outs
    return (yp, ys, kv_a[None], kv_b[None], new_a[None], new_b[None])
```

```python
import functools

import jax
import jax.numpy as jnp
import numpy as np
from jax import lax
from jax.experimental import pallas as pl
from jax.experimental.pallas import tpu as pltpu

D_MODEL = 1024
HEAD_DIM = 64
A_HEADS = 8
B_HEADS = 8
B_KV_HEADS = 2
DILATIONS = (1, 4, 16)
BAND = 128
A_WINDOW = 2048
B_WINDOW = 128
PAST_LEN = 16384
ROT_DIM = HEAD_DIM // 4
ROPE_THETA = 500000.0
LN_EPS = 1e-5
A_W = A_HEADS * HEAD_DIM
B_W = B_HEADS * HEAD_DIM
B_KV_W = B_KV_HEADS * HEAD_DIM
LANES = 128
HQ_W = 3 * A_W + B_W + 2 * B_KV_W
G_W = A_W + B_W
IN_WIDTH = HQ_W + G_W
QA_T, KA_T, VA_T = 0, A_W // LANES, 2 * A_W // LANES
QB_T = 3 * A_W // LANES
KB_T = QB_T + B_W // LANES
VB_T = KB_T + 1
VMEM_LIMIT = 56 * 1024 * 1024
COL_CHUNK = 256
NEG_INF = float("-inf")
BLOCKS_IN_FLIGHT = 4


def _proj_kernel(x_ref, w_ref, cos_ref, sinp_ref, sinm_ref, hq_ref, sg_ref):
    x = x_ref[...].astype(jnp.bfloat16)
    cos = cos_ref[...]
    sinp = sinp_ref[...]
    sinm = sinm_ref[...]
    rope_tiles = set(range(QA_T, VA_T)) | set(range(QB_T, VB_T))
    for c in range(IN_WIDTH // COL_CHUNK):
        acc = jnp.dot(x, w_ref[:, c * COL_CHUNK:(c + 1) * COL_CHUNK],
                      preferred_element_type=jnp.float32)
        for half in range(COL_CHUNK // LANES):
            tile = c * (COL_CHUNK // LANES) + half
            t = acc[:, half * LANES:(half + 1) * LANES]
            if tile < HQ_W // LANES:
                if tile in rope_tiles:
                    t = (t * cos + pltpu.roll(t, ROT_DIM // 2, 1) * sinp
                         + pltpu.roll(t, LANES - ROT_DIM // 2, 1) * sinm)
                hq_ref[:, tile * LANES:(tile + 1) * LANES] = t
            else:
                g = tile - HQ_W // LANES
                sg_ref[:, g * LANES:(g + 1) * LANES] = (t * jax.nn.sigmoid(t)).astype(sg_ref.dtype)


def _project(x2, w, cos, sinp, sinm, tm):
    n = x2.shape[0]
    tab_blocks = cos.shape[0] // tm
    row = lambda i: (i, 0)
    tab = lambda i: (i % tab_blocks, 0)
    return pl.pallas_call(
        _proj_kernel,
        grid=(n // tm,),
        in_specs=[
            pl.BlockSpec((tm, D_MODEL), row),
            pl.BlockSpec((D_MODEL, IN_WIDTH), lambda i: (0, 0)),
            pl.BlockSpec((tm, LANES), tab),
            pl.BlockSpec((tm, LANES), tab),
            pl.BlockSpec((tm, LANES), tab),
        ],
        out_specs=[pl.BlockSpec((tm, HQ_W), row), pl.BlockSpec((tm, G_W), row)],
        out_shape=[jax.ShapeDtypeStruct((n, HQ_W), jnp.float32),
                   jax.ShapeDtypeStruct((n, G_W), jnp.bfloat16)],
        compiler_params=pltpu.CompilerParams(
            dimension_semantics=("arbitrary",), vmem_limit_bytes=VMEM_LIMIT),
        name="proj",
    )(x2, w, cos, sinp, sinm)


def _lane_lo(shape):
    return lax.broadcasted_iota(jnp.int32, shape, len(shape) - 1) < HEAD_DIM


def _attn_kernel(*refs, dilations, seq, with_sink):
    if with_sink:
        q_ref, k_ref, v_ref, bias_ref, sink_ref, o_ref = refs[:6]
        scratch = refs[6:]
    else:
        q_ref, k_ref, v_ref, bias_ref, o_ref = refs[:5]
        scratch = refs[5:]
    multi = len(dilations) > 1
    if multi:
        acc_ref, m_ref, l_ref = scratch
    lo = _lane_lo((BAND, LANES))
    ones = jnp.ones((2 * BAND, LANES), jnp.bfloat16)

    def pair(a, b):
        return jnp.where(lo, a, b)

    for ci, d in enumerate(dilations):
        first = ci == 0
        last = ci == len(dilations) - 1
        nblk = seq // d // BAND

        def rows(start, d=d):
            return pl.ds(start, BAND) if d == 1 else pl.ds(start, BAND, stride=d)

        def block(i, carry, r, d=d, first=first, last=last, rows=rows):
            kp, vp = carry
            start = r + i * (BAND * d)
            q = q_ref[rows(start), :]
            kc = k_ref[rows(start), :].astype(jnp.bfloat16)
            vc = v_ref[rows(start), :].astype(jnp.bfloat16)
            q2 = jnp.concatenate([jnp.where(lo, q, 0.0), jnp.where(lo, 0.0, q)],
                                 axis=0).astype(jnp.bfloat16)
            kk = jnp.concatenate([kp, kc], axis=0)
            s = lax.dot_general(q2, kk, (((1,), (1,)), ((), ())),
                                preferred_element_type=jnp.float32)
            s = s + bias_ref[jnp.minimum(i, 1)]
            m = jnp.max(s, axis=1, keepdims=True)
            p = jnp.exp(s - m).astype(jnp.bfloat16)
            vv = jnp.concatenate([jnp.concatenate([vp, vc], axis=0), ones], axis=1)
            pv = jnp.dot(p, vv, preferred_element_type=jnp.float32)
            o_n = pair(pv[:BAND, :LANES], pv[BAND:, :LANES])
            l_n = pair(pv[:BAND, LANES:], pv[BAND:, LANES:])
            m_n = pair(jnp.broadcast_to(m[:BAND], (BAND, LANES)),
                       jnp.broadcast_to(m[BAND:], (BAND, LANES)))
            if multi and not first:
                m_o = m_ref[rows(start), :]
                m_t = jnp.maximum(m_o, m_n)
                a_o = jnp.exp(m_o - m_t)
                a_n = jnp.exp(m_n - m_t)
                o_n = acc_ref[rows(start), :] * a_o + o_n * a_n
                l_n = l_ref[rows(start), :] * a_o + l_n * a_n
                m_n = m_t
            if last:
                if with_sink:
                    sink = sink_ref[...]
                    m_t = jnp.maximum(m_n, sink)
                    a_n = jnp.exp(m_n - m_t)
                    o_n = o_n * a_n
                    l_n = l_n * a_n + jnp.exp(sink - m_t)
                o_ref[rows(start), :] = (o_n / l_n).astype(o_ref.dtype)
            else:
                acc_ref[rows(start), :] = o_n
                l_ref[rows(start), :] = l_n
                m_ref[rows(start), :] = m_n
            return kc, vc

        zero = jnp.zeros((BAND, LANES), jnp.bfloat16)

        inner = min(nblk, BLOCKS_IN_FLIGHT)

        def residue(r, _, nblk=nblk, block=block, inner=inner):
            lax.fori_loop(0, nblk, functools.partial(block, r=r), (zero, zero), unroll=inner)
            return 0

        if d == 1:
            residue(0, 0)
        else:
            lax.fori_loop(0, d, residue, 0, unroll=BLOCKS_IN_FLIGHT // inner)


def _band_bias():
    qi = np.arange(BAND)[:, None]
    ki = np.arange(2 * BAND)[None, :]
    dist = BAND + qi - ki
    band = (dist >= 0) & (dist <= BAND)
    full = np.where(band, 0.0, NEG_INF).astype(np.float32)
    head = np.where(band & (ki >= BAND), 0.0, NEG_INF).astype(np.float32)
    both = np.stack([head, full])
    return jnp.asarray(np.concatenate([both, both], axis=1))


def _attention(hq3, q_tile, k_tile, v_tile, shared_kv, dilations, sink=None):
    b, seq, _ = hq3.shape
    n_pairs = A_W // LANES
    with_sink = sink is not None
    kv_off = (lambda u: 0) if shared_kv else (lambda u: u)
    in_specs = [
        pl.BlockSpec((None, seq, LANES), lambda i, u: (i, 0, q_tile + u)),
        pl.BlockSpec((None, seq, LANES), lambda i, u: (i, 0, k_tile + kv_off(u))),
        pl.BlockSpec((None, seq, LANES), lambda i, u: (i, 0, v_tile + kv_off(u))),
        pl.BlockSpec((2, 2 * BAND, 2 * BAND), lambda i, u: (0, 0, 0)),
    ]
    args = [hq3, hq3, hq3, _band_bias()]
    if with_sink:
        in_specs.append(pl.BlockSpec((None, 1, LANES), lambda i, u: (u, 0, 0)))
        args.append(sink)
    scratch = []
    if len(dilations) > 1:
        scratch = [pltpu.VMEM((seq, LANES), jnp.float32)] * 3
    return pl.pallas_call(
        functools.partial(_attn_kernel, dilations=dilations, seq=seq, with_sink=with_sink),
        grid=(b, n_pairs),
        in_specs=in_specs,
        out_specs=pl.BlockSpec((None, seq, LANES), lambda i, u: (i, 0, u)),
        out_shape=jax.ShapeDtypeStruct((b, seq, n_pairs * LANES), jnp.bfloat16),
        scratch_shapes=scratch,
        compiler_params=pltpu.CompilerParams(
            dimension_semantics=("arbitrary", "arbitrary"), vmem_limit_bytes=VMEM_LIMIT),
        name="attn_b" if with_sink else "attn_a",
    )(*args)


def _sample_kernel(hq_ref, ca_ref, cb_ref, w_ref, sink_ref, oa_ref, ob_ref):
    row = hq_ref[...]
    qa = row[:, QA_T * LANES:KA_T * LANES]
    ka_new = row[:, KA_T * LANES:VA_T * LANES]
    va_new = row[:, VA_T * LANES:QB_T * LANES]
    qb = row[:, QB_T * LANES:KB_T * LANES]
    kb_new = row[:, KB_T * LANES:VB_T * LANES]
    vb_new = row[:, VB_T * LANES:HQ_W]

    hrow = lax.broadcasted_iota(jnp.int32, (A_HEADS, A_W), 0)
    hcol = lax.broadcasted_iota(jnp.int32, (A_HEADS, A_W), 1) // HEAD_DIM
    own = hrow == hcol
    q_bd = jnp.where(own, jnp.broadcast_to(qa, (A_HEADS, A_W)), 0.0)
    s = jnp.dot(q_bd.astype(jnp.bfloat16), ca_ref[0].astype(jnp.bfloat16),
                preferred_element_type=jnp.float32)
    s_new = jnp.sum(q_bd * ka_new, axis=1, keepdims=True)
    w = w_ref[...]
    s = jnp.where(w > 0.0, s, NEG_INF)
    m = jnp.maximum(jnp.max(s, axis=1, keepdims=True), s_new)
    p = jnp.exp(s - m) * w
    p_new = float(len(DILATIONS)) * jnp.exp(s_new - m)
    den = jnp.sum(p, axis=1, keepdims=True) + p_new
    o = lax.dot_general(p.astype(jnp.bfloat16), ca_ref[1].astype(jnp.bfloat16),
                        (((1,), (1,)), ((), ())), preferred_element_type=jnp.float32)
    o = (o + p_new * va_new) / den
    oa_ref[...] = jnp.sum(jnp.where(own, o, 0.0), axis=0, keepdims=True)

    prow = lax.broadcasted_iota(jnp.int32, (B_HEADS, LANES), 0)
    pcol = lax.broadcasted_iota(jnp.int32, (B_HEADS, LANES), 1) // HEAD_DIM
    qb_rows = jnp.zeros((B_HEADS, LANES), jnp.float32)
    for t in range(B_W // LANES):
        qb_rows = jnp.where(prow // 2 == t,
                            jnp.broadcast_to(qb[:, t * LANES:(t + 1) * LANES], (B_HEADS, LANES)), qb_rows)
    own_b = (prow % 2) == pcol
    qb_bd = jnp.where(own_b, qb_rows, 0.0)
    sb = jnp.dot(qb_bd.astype(jnp.bfloat16), cb_ref[0].astype(jnp.bfloat16),
                 preferred_element_type=jnp.float32)
    sb_new = jnp.sum(qb_bd * kb_new, axis=1, keepdims=True)
    sink = sink_ref[...]
    mb = jnp.maximum(jnp.maximum(jnp.max(sb, axis=1, keepdims=True), sb_new), sink)
    pb = jnp.exp(sb - mb)
    pb_new = jnp.exp(sb_new - mb)
    den_b = jnp.sum(pb, axis=1, keepdims=True) + pb_new + jnp.exp(sink - mb)
    ob = lax.dot_general(pb.astype(jnp.bfloat16), cb_ref[1].astype(jnp.bfloat16),
                         (((1,), (1,)), ((), ())), preferred_element_type=jnp.float32)
    ob = (ob + pb_new * vb_new) / den_b
    lo = _lane_lo((1, LANES))
    for t in range(B_W // LANES):
        ob_ref[:, t * LANES:(t + 1) * LANES] = jnp.where(lo, ob[2 * t:2 * t + 1], ob[2 * t + 1:2 * t + 2])


def _sample_weights(buf):
    dist = buf - np.arange(buf)
    w = np.zeros((buf,), np.float32)
    for d in DILATIONS:
        w += ((dist % d == 0) & (dist <= BAND * d)).astype(np.float32)
    return jnp.asarray(w[None, :])


def _sample_attention(hq_s, cache_a_t, cache_b_t, sink_col):
    n = hq_s.shape[0]
    buf_a = cache_a_t.shape[-1]
    buf_b = cache_b_t.shape[-1]
    return pl.pallas_call(
        _sample_kernel,
        grid=(n,),
        in_specs=[
            pl.BlockSpec((None, 1, HQ_W), lambda i: (i, 0, 0)),
            pl.BlockSpec((None, 2, A_W, buf_a), lambda i: (i, 0, 0, 0)),
            pl.BlockSpec((None, 2, B_KV_W, buf_b), lambda i: (i, 0, 0, 0)),
            pl.BlockSpec((1, buf_a), lambda i: (0, 0)),
            pl.BlockSpec((B_HEADS, 1), lambda i: (0, 0)),
        ],
        out_specs=[pl.BlockSpec((None, 1, A_W), lambda i: (i, 0, 0)),
                   pl.BlockSpec((None, 1, B_W), lambda i: (i, 0, 0))],
        out_shape=[jax.ShapeDtypeStruct((n, 1, A_W), jnp.float32),
                   jax.ShapeDtypeStruct((n, 1, B_W), jnp.float32)],
        compiler_params=pltpu.CompilerParams(
            dimension_semantics=("arbitrary",), vmem_limit_bytes=VMEM_LIMIT),
        name="sample_attn",
    )(hq_s.reshape(n, 1, HQ_W), cache_a_t, cache_b_t, _sample_weights(buf_a), sink_col)


def _out_kernel(x_ref, oa_ref, ob_ref, sg_ref, wo_ref, g_ref, b_ref, y_ref, *, alpha):
    sg = sg_ref[...].astype(jnp.float32)
    mix_a = (oa_ref[...].astype(jnp.float32) * sg[:, :A_W]).astype(jnp.bfloat16)
    mix_b = (ob_ref[...].astype(jnp.float32) * sg[:, A_W:]).astype(jnp.bfloat16)
    out = (jnp.dot(mix_a, wo_ref[:A_W, :], preferred_element_type=jnp.float32)
           + jnp.dot(mix_b, wo_ref[A_W:, :], preferred_element_type=jnp.float32))
    z = alpha * x_ref[...] + out
    mu = jnp.mean(z, axis=1, keepdims=True)
    zc = z - mu
    var = jnp.mean(zc * zc, axis=1, keepdims=True)
    y_ref[...] = zc * lax.rsqrt(var + LN_EPS) * g_ref[...] + b_ref[...]


def _output(x2, oa, ob, sg, wo, g, b, alpha, tm):
    n = x2.shape[0]
    row = lambda i: (i, 0)
    fixed = lambda i: (0, 0)
    return pl.pallas_call(
        functools.partial(_out_kernel, alpha=alpha),
        grid=(n // tm,),
        in_specs=[
            pl.BlockSpec((tm, D_MODEL), row),
            pl.BlockSpec((tm, A_W), row),
            pl.BlockSpec((tm, B_W), row),
            pl.BlockSpec((tm, G_W), row),
            pl.BlockSpec((G_W, D_MODEL), fixed),
            pl.BlockSpec((1, D_MODEL), fixed),
            pl.BlockSpec((1, D_MODEL), fixed),
        ],
        out_specs=pl.BlockSpec((tm, D_MODEL), row),
        out_shape=jax.ShapeDtypeStruct((n, D_MODEL), jnp.float32),
        compiler_params=pltpu.CompilerParams(
            dimension_semantics=("arbitrary",), vmem_limit_bytes=VMEM_LIMIT),
        name="out_proj",
    )(x2, oa, ob, sg, wo, g, b)


def _rope_tables(pos):
    inv = ROPE_THETA ** (-jnp.arange(0, ROT_DIM, 2, dtype=jnp.float32) / ROT_DIM)
    ang = pos.astype(jnp.float32)[:, None] * inv[None, :]
    cos = jnp.cos(ang)
    sin = jnp.sin(ang)
    n = pos.shape[0]
    half = ROT_DIM // 2
    pad = HEAD_DIM - ROT_DIM
    one = jnp.ones((n, pad), jnp.float32)
    zero = jnp.zeros((n, pad), jnp.float32)
    zh = jnp.zeros((n, half), jnp.float32)
    cos_h = jnp.concatenate([cos, cos, one], axis=1)
    sinp_h = jnp.concatenate([zh, sin, zero], axis=1)
    sinm_h = jnp.concatenate([-sin, zh, zero], axis=1)
    rep = LANES // HEAD_DIM
    return tuple(jnp.tile(t, (1, rep)) for t in (cos_h, sinp_h, sinm_h))


def _b_head_perm():
    group = B_HEADS // B_KV_HEADS
    cols = []
    for t in range(group):
        for e in range(B_KV_HEADS):
            g = t + group * e
            cols.extend(range(g * HEAD_DIM, (g + 1) * HEAD_DIM))
    return np.asarray(cols, np.int32)


def _layer(xp, xs, cache_a, cache_b, w_in, sinks, w_o, ln_g, ln_b, alpha, past_len):
    b, seq, _ = xp.shape
    nb = xs.shape[0]
    perm = _b_head_perm()
    o_qa, o_ka, o_va, o_ga = 0, A_W, 2 * A_W, 3 * A_W
    o_qb = 4 * A_W
    o_kb = o_qb + B_W
    o_vb = o_kb + B_KV_W
    o_gb = o_vb + B_KV_W
    ar = np.arange
    cols = np.concatenate([ar(o_qa, o_qa + A_W), ar(o_ka, o_ka + A_W), ar(o_va, o_va + A_W),
                           o_qb + perm, ar(o_kb, o_kb + B_KV_W), ar(o_vb, o_vb + B_KV_W),
                           ar(o_ga, o_ga + A_W), o_gb + perm]).astype(np.int32)
    scale = np.ones((IN_WIDTH,), np.float32)
    scale[:A_W] = HEAD_DIM ** -0.5
    scale[QB_T * LANES:KB_T * LANES] = HEAD_DIM ** -0.5
    w = (w_in[:, cols] * scale[None, :]).astype(jnp.bfloat16)
    rows_o = np.concatenate([ar(A_W), A_W + perm]).astype(np.int32)
    wo = w_o[rows_o, :].astype(jnp.bfloat16)
    sink_p = sinks[perm[::HEAD_DIM] // HEAD_DIM].astype(jnp.float32)
    sink_tiles = jnp.repeat(sink_p, HEAD_DIM).reshape(B_W // LANES, 1, LANES)
    g2 = ln_g.reshape(1, D_MODEL)
    b2 = ln_b.reshape(1, D_MODEL)

    tm = 512
    x2 = xp.reshape(b * seq, D_MODEL)
    hq, sg = _project(x2, w, *_rope_tables(jnp.arange(seq)), tm)
    hq3 = hq.reshape(b, seq, HQ_W)
    oa = _attention(hq3, QA_T, KA_T, VA_T, False, tuple(sorted(DILATIONS, reverse=True)))
    ob = _attention(hq3, QB_T, KB_T, VB_T, True, (1,), sink=sink_tiles)
    y = _output(x2, oa.reshape(b * seq, A_W), ob.reshape(b * seq, B_W), sg, wo, g2, b2, alpha, tm)
    wa = min(A_WINDOW, seq)
    wb = min(B_WINDOW, seq)
    ka = hq3[:, seq - wa:, KA_T * LANES:VA_T * LANES].reshape(b, wa, A_HEADS, HEAD_DIM)
    va = hq3[:, seq - wa:, VA_T * LANES:QB_T * LANES].reshape(b, wa, A_HEADS, HEAD_DIM)
    kb = hq3[:, seq - wb:, KB_T * LANES:VB_T * LANES].reshape(b, wb, B_KV_HEADS, HEAD_DIM)
    vb = hq3[:, seq - wb:, VB_T * LANES:HQ_W].reshape(b, wb, B_KV_HEADS, HEAD_DIM)
    kv_a = jnp.stack([ka, va], axis=2)
    kv_b = jnp.stack([kb, vb], axis=2)

    xs2 = xs.reshape(nb, D_MODEL)
    pos_s = jnp.full((nb,), past_len, jnp.int32)
    hq_s, sg_s = _project(xs2, w, *_rope_tables(pos_s), nb)
    ca_t = jnp.transpose(cache_a, (0, 2, 3, 4, 1)).reshape(nb, 2, A_W, cache_a.shape[1])
    cb_t = jnp.transpose(cache_b, (0, 2, 3, 4, 1)).reshape(nb, 2, B_KV_W, cache_b.shape[1])
    oa_s, ob_s = _sample_attention(hq_s, ca_t, cb_t, sink_p.reshape(B_HEADS, 1))
    y_s = _output(xs2, oa_s.reshape(nb, A_W), ob_s.reshape(nb, B_W), sg_s, wo, g2, b2, alpha, nb)
    new_a = jnp.stack([hq_s[:, KA_T * LANES:VA_T * LANES].reshape(nb, 1, A_HEADS, HEAD_DIM),
                       hq_s[:, VA_T * LANES:QB_T * LANES].reshape(nb, 1, A_HEADS, HEAD_DIM)], axis=2)
    new_b = jnp.stack([hq_s[:, KB_T * LANES:VB_T * LANES].reshape(nb, 1, B_KV_HEADS, HEAD_DIM),
                       hq_s[:, VB_T * LANES:HQ_W].reshape(nb, 1, B_KV_HEADS, HEAD_DIM)], axis=2)
    return (y.reshape(b, seq, D_MODEL), y_s.reshape(nb, 1, D_MODEL), kv_a, kv_b, new_a, new_b)


def kernel(x_prompt, x_sample, cache_a_kv, cache_b_kv, w_in, attn_sinks, w_o, ln_g, ln_b):
    depth = w_in.shape[0]
    assert depth == 1 and x_sample.shape[1] == 1, "single layer, one sample token per sequence"
    assert cache_a_kv.shape[2] == BAND * max(DILATIONS), "mixer-A cache covers every strided read"
    assert cache_b_kv.shape[2] == B_WINDOW
    alpha = (2 * depth) ** 0.25
    outs = _layer(x_prompt, x_sample, cache_a_kv[0], cache_b_kv[0], w_in[0], attn_sinks[0],
                  w_o[0], ln_g[0], ln_b[0], alpha, PAST_LEN)
    yp, ys, kv_a, kv_b, new_a, new_b = outs
    return (yp, ys, kv_a[None], kv_b[None], new_a[None], new_b[None])
```

```python
import functools

import jax
import jax.numpy as jnp
import numpy as np
from jax import lax
from jax.experimental import pallas as pl
from jax.experimental.pallas import tpu as pltpu

D_MODEL = 1024
HEAD_DIM = 64
A_HEADS = 8
B_HEADS = 8
B_KV_HEADS = 2
DILATIONS = (1, 4, 16)
BAND = 128
A_WINDOW = 2048
B_WINDOW = 128
PAST_LEN = 16384
ROT_DIM = HEAD_DIM // 4
ROPE_THETA = 500000.0
LN_EPS = 1e-5
A_W = A_HEADS * HEAD_DIM
B_W = B_HEADS * HEAD_DIM
B_KV_W = B_KV_HEADS * HEAD_DIM
LANES = 128
HQ_W = 3 * A_W + B_W + 2 * B_KV_W
G_W = A_W + B_W
IN_WIDTH = HQ_W + G_W
QA_T, KA_T, VA_T = 0, A_W // LANES, 2 * A_W // LANES
QB_T = 3 * A_W // LANES
KB_T = QB_T + B_W // LANES
VB_T = KB_T + 1
VMEM_LIMIT = 56 * 1024 * 1024
COL_CHUNK = 256
NEG_INF = float("-inf")
BLOCKS_IN_FLIGHT = 8


def _proj_kernel(x_ref, w_ref, cos_ref, sinp_ref, sinm_ref, hq_ref, sg_ref, kva_ref, kvb_ref,
                 *, tiles_per_seq, first_a_tile, wb):
    tm = x_ref.shape[0]
    x = x_ref[...].astype(jnp.bfloat16)
    cos = cos_ref[...]
    sinp = sinp_ref[...]
    sinm = sinm_ref[...]
    rope_tiles = set(range(QA_T, VA_T)) | set(range(QB_T, VB_T))
    for c in range(IN_WIDTH // COL_CHUNK):
        acc = jnp.dot(x, w_ref[:, c * COL_CHUNK:(c + 1) * COL_CHUNK],
                      preferred_element_type=jnp.float32)
        for half in range(COL_CHUNK // LANES):
            tile = c * (COL_CHUNK // LANES) + half
            t = acc[:, half * LANES:(half + 1) * LANES]
            if tile < HQ_W // LANES:
                if tile in rope_tiles:
                    t = (t * cos + pltpu.roll(t, ROT_DIM // 2, 1) * sinp
                         + pltpu.roll(t, LANES - ROT_DIM // 2, 1) * sinm)
                hq_ref[:, tile * LANES:(tile + 1) * LANES] = t
            else:
                g = tile - HQ_W // LANES
                sg_ref[:, g * LANES:(g + 1) * LANES] = (t * jax.nn.sigmoid(t)).astype(sg_ref.dtype)

    seq_tile = pl.program_id(0) % tiles_per_seq

    def emit_a():
        for kv, first in enumerate((KA_T, VA_T)):
            for j in range(A_W // LANES):
                cols = slice((first + j) * LANES, (first + j + 1) * LANES)
                kva_ref[kv, j * LANES:(j + 1) * LANES, :] = hq_ref[:, cols].T

    def emit_b():
        for kv, tile in enumerate((KB_T, VB_T)):
            kvb_ref[kv] = hq_ref[tm - wb:, tile * LANES:(tile + 1) * LANES].T

    if tiles_per_seq == 1:
        emit_a()
        emit_b()
    else:
        pl.when(seq_tile >= first_a_tile)(emit_a)
        pl.when(seq_tile == tiles_per_seq - 1)(emit_b)


def _project(x2, w, cos, sinp, sinm, tm, seq, wa, wb):
    n = x2.shape[0]
    tab_blocks = cos.shape[0] // tm
    tps = seq // tm
    first_a = tps - wa // tm
    assert seq % tm == 0 and wa % tm == 0 and wb <= tm and cos.shape[0] == seq
    row = lambda i: (i, 0)
    tab = lambda i: (i % tab_blocks, 0)
    return pl.pallas_call(
        functools.partial(_proj_kernel, tiles_per_seq=tps, first_a_tile=first_a, wb=wb),
        grid=(n // tm,),
        in_specs=[
            pl.BlockSpec((tm, D_MODEL), row),
            pl.BlockSpec((D_MODEL, IN_WIDTH), lambda i: (0, 0)),
            pl.BlockSpec((tm, LANES), tab),
            pl.BlockSpec((tm, LANES), tab),
            pl.BlockSpec((tm, LANES), tab),
        ],
        out_specs=[
            pl.BlockSpec((tm, HQ_W), row),
            pl.BlockSpec((tm, G_W), row),
            pl.BlockSpec((None, 2, A_W, tm),
                         lambda i: (i // tps, 0, 0, jnp.maximum(i % tps - first_a, 0))),
            pl.BlockSpec((None, 2, B_KV_W, wb), lambda i: (i // tps, 0, 0, 0)),
        ],
        out_shape=[jax.ShapeDtypeStruct((n, HQ_W), jnp.float32),
                   jax.ShapeDtypeStruct((n, G_W), jnp.bfloat16),
                   jax.ShapeDtypeStruct((n // seq, 2, A_W, wa), jnp.float32),
                   jax.ShapeDtypeStruct((n // seq, 2, B_KV_W, wb), jnp.float32)],
        compiler_params=pltpu.CompilerParams(
            dimension_semantics=("arbitrary",), vmem_limit_bytes=VMEM_LIMIT),
        name="proj",
    )(x2, w, cos, sinp, sinm)


def _lane_lo(shape):
    return lax.broadcasted_iota(jnp.int32, shape, len(shape) - 1) < HEAD_DIM


def _attn_kernel(*refs, dilations, seq, with_sink):
    if with_sink:
        q_ref, k_ref, v_ref, bias_ref, sink_ref, o_ref = refs[:6]
        scratch = refs[6:]
    else:
        q_ref, k_ref, v_ref, bias_ref, o_ref = refs[:5]
        scratch = refs[5:]
    multi = len(dilations) > 1
    if multi:
        acc_ref, m_ref, l_ref = scratch
    lo = _lane_lo((BAND, LANES))
    ones = jnp.ones((2 * BAND, LANES), jnp.bfloat16)

    def pair(a, b):
        return jnp.where(lo, a, b)

    for ci, d in enumerate(dilations):
        first = ci == 0
        last = ci == len(dilations) - 1
        nblk = seq // d // BAND

        def rows(start, d=d):
            return pl.ds(start, BAND) if d == 1 else pl.ds(start, BAND, stride=d)

        def block(i, carry, r, d=d, first=first, last=last, rows=rows):
            kp, vp = carry
            start = r + i * (BAND * d)
            q = q_ref[rows(start), :]
            kc = k_ref[rows(start), :].astype(jnp.bfloat16)
            vc = v_ref[rows(start), :].astype(jnp.bfloat16)
            q2 = jnp.concatenate([jnp.where(lo, q, 0.0), jnp.where(lo, 0.0, q)],
                                 axis=0).astype(jnp.bfloat16)
            kk = jnp.concatenate([kp, kc], axis=0)
            s = lax.dot_general(q2, kk, (((1,), (1,)), ((), ())),
                                preferred_element_type=jnp.float32)
            s = s + bias_ref[jnp.minimum(i, 1)]
            m = jnp.max(s, axis=1, keepdims=True)
            p = jnp.exp(s - m).astype(jnp.bfloat16)
            vv = jnp.concatenate([jnp.concatenate([vp, vc], axis=0), ones], axis=1)
            pv = jnp.dot(p, vv, preferred_element_type=jnp.float32)
            o_n = pair(pv[:BAND, :LANES], pv[BAND:, :LANES])
            l_n = pair(pv[:BAND, LANES:], pv[BAND:, LANES:])
            m_n = pair(jnp.broadcast_to(m[:BAND], (BAND, LANES)),
                       jnp.broadcast_to(m[BAND:], (BAND, LANES)))
            if multi and not first:
                m_o = m_ref[rows(start), :]
                m_t = jnp.maximum(m_o, m_n)
                a_o = jnp.exp(m_o - m_t)
                a_n = jnp.exp(m_n - m_t)
                o_n = acc_ref[rows(start), :] * a_o + o_n * a_n
                l_n = l_ref[rows(start), :] * a_o + l_n * a_n
                m_n = m_t
            if last:
                if with_sink:
                    sink = sink_ref[...]
                    m_t = jnp.maximum(m_n, sink)
                    a_n = jnp.exp(m_n - m_t)
                    o_n = o_n * a_n
                    l_n = l_n * a_n + jnp.exp(sink - m_t)
                o_ref[rows(start), :] = (o_n / l_n).astype(o_ref.dtype)
            else:
                acc_ref[rows(start), :] = o_n
                l_ref[rows(start), :] = l_n
                m_ref[rows(start), :] = m_n
            return kc, vc

        zero = jnp.zeros((BAND, LANES), jnp.bfloat16)

        inner = min(nblk, BLOCKS_IN_FLIGHT)

        def residue(r, _, nblk=nblk, block=block, inner=inner):
            lax.fori_loop(0, nblk, functools.partial(block, r=r), (zero, zero), unroll=inner)
            return 0

        if d == 1:
            residue(0, 0)
        else:
            lax.fori_loop(0, d, residue, 0, unroll=BLOCKS_IN_FLIGHT // inner)


def _band_bias():
    qi = np.arange(BAND)[:, None]
    ki = np.arange(2 * BAND)[None, :]
    dist = BAND + qi - ki
    band = (dist >= 0) & (dist <= BAND)
    full = np.where(band, 0.0, NEG_INF).astype(np.float32)
    head = np.where(band & (ki >= BAND), 0.0, NEG_INF).astype(np.float32)
    both = np.stack([head, full])
    return jnp.asarray(np.concatenate([both, both], axis=1))


def _attention(hq3, q_tile, k_tile, v_tile, shared_kv, dilations, sink=None):
    b, seq, _ = hq3.shape
    n_pairs = A_W // LANES
    with_sink = sink is not None
    kv_off = (lambda u: 0) if shared_kv else (lambda u: u)
    in_specs = [
        pl.BlockSpec((None, seq, LANES), lambda i, u: (i, 0, q_tile + u)),
        pl.BlockSpec((None, seq, LANES), lambda i, u: (i, 0, k_tile + kv_off(u))),
        pl.BlockSpec((None, seq, LANES), lambda i, u: (i, 0, v_tile + kv_off(u))),
        pl.BlockSpec((2, 2 * BAND, 2 * BAND), lambda i, u: (0, 0, 0)),
    ]
    args = [hq3, hq3, hq3, _band_bias()]
    if with_sink:
        in_specs.append(pl.BlockSpec((None, 1, LANES), lambda i, u: (u, 0, 0)))
        args.append(sink)
    scratch = []
    if len(dilations) > 1:
        scratch = [pltpu.VMEM((seq, LANES), jnp.float32)] * 3
    return pl.pallas_call(
        functools.partial(_attn_kernel, dilations=dilations, seq=seq, with_sink=with_sink),
        grid=(b, n_pairs),
        in_specs=in_specs,
        out_specs=pl.BlockSpec((None, seq, LANES), lambda i, u: (i, 0, u)),
        out_shape=jax.ShapeDtypeStruct((b, seq, n_pairs * LANES), jnp.bfloat16),
        scratch_shapes=scratch,
        compiler_params=pltpu.CompilerParams(
            dimension_semantics=("arbitrary", "arbitrary"), vmem_limit_bytes=VMEM_LIMIT),
        name="attn_b" if with_sink else "attn_a",
    )(*args)


def _sample_kernel(hq_ref, ca_ref, cb_ref, w_ref, sink_ref, oa_ref, ob_ref):
    row = hq_ref[...]
    qa = row[:, QA_T * LANES:KA_T * LANES]
    ka_new = row[:, KA_T * LANES:VA_T * LANES]
    va_new = row[:, VA_T * LANES:QB_T * LANES]
    qb = row[:, QB_T * LANES:KB_T * LANES]
    kb_new = row[:, KB_T * LANES:VB_T * LANES]
    vb_new = row[:, VB_T * LANES:HQ_W]

    hrow = lax.broadcasted_iota(jnp.int32, (A_HEADS, A_W), 0)
    hcol = lax.broadcasted_iota(jnp.int32, (A_HEADS, A_W), 1) // HEAD_DIM
    own = hrow == hcol
    q_bd = jnp.where(own, jnp.broadcast_to(qa, (A_HEADS, A_W)), 0.0)
    s = jnp.dot(q_bd.astype(jnp.bfloat16), ca_ref[0].astype(jnp.bfloat16),
                preferred_element_type=jnp.float32)
    s_new = jnp.sum(q_bd * ka_new, axis=1, keepdims=True)
    w = w_ref[...]
    s = jnp.where(w > 0.0, s, NEG_INF)
    m = jnp.maximum(jnp.max(s, axis=1, keepdims=True), s_new)
    p = jnp.exp(s - m) * w
    p_new = float(len(DILATIONS)) * jnp.exp(s_new - m)
    den = jnp.sum(p, axis=1, keepdims=True) + p_new
    o = lax.dot_general(p.astype(jnp.bfloat16), ca_ref[1].astype(jnp.bfloat16),
                        (((1,), (1,)), ((), ())), preferred_element_type=jnp.float32)
    o = (o + p_new * va_new) / den
    oa_ref[...] = jnp.sum(jnp.where(own, o, 0.0), axis=0, keepdims=True)

    prow = lax.broadcasted_iota(jnp.int32, (B_HEADS, LANES), 0)
    pcol = lax.broadcasted_iota(jnp.int32, (B_HEADS, LANES), 1) // HEAD_DIM
    qb_rows = jnp.zeros((B_HEADS, LANES), jnp.float32)
    for t in range(B_W // LANES):
        qb_rows = jnp.where(prow // 2 == t,
                            jnp.broadcast_to(qb[:, t * LANES:(t + 1) * LANES], (B_HEADS, LANES)), qb_rows)
    own_b = (prow % 2) == pcol
    qb_bd = jnp.where(own_b, qb_rows, 0.0)
    sb = jnp.dot(qb_bd.astype(jnp.bfloat16), cb_ref[0].astype(jnp.bfloat16),
                 preferred_element_type=jnp.float32)
    sb_new = jnp.sum(qb_bd * kb_new, axis=1, keepdims=True)
    sink = sink_ref[...]
    mb = jnp.maximum(jnp.maximum(jnp.max(sb, axis=1, keepdims=True), sb_new), sink)
    pb = jnp.exp(sb - mb)
    pb_new = jnp.exp(sb_new - mb)
    den_b = jnp.sum(pb, axis=1, keepdims=True) + pb_new + jnp.exp(sink - mb)
    ob = lax.dot_general(pb.astype(jnp.bfloat16), cb_ref[1].astype(jnp.bfloat16),
                         (((1,), (1,)), ((), ())), preferred_element_type=jnp.float32)
    ob = (ob + pb_new * vb_new) / den_b
    lo = _lane_lo((1, LANES))
    for t in range(B_W // LANES):
        ob_ref[:, t * LANES:(t + 1) * LANES] = jnp.where(lo, ob[2 * t:2 * t + 1], ob[2 * t + 1:2 * t + 2])


def _sample_weights(buf):
    dist = buf - np.arange(buf)
    w = np.zeros((buf,), np.float32)
    for d in DILATIONS:
        w += ((dist % d == 0) & (dist <= BAND * d)).astype(np.float32)
    return jnp.asarray(w[None, :])


def _sample_attention(hq_s, cache_a_t, cache_b_t, sink_col):
    n = hq_s.shape[0]
    buf_a = cache_a_t.shape[-1]
    buf_b = cache_b_t.shape[-1]
    return pl.pallas_call(
        _sample_kernel,
        grid=(n,),
        in_specs=[
            pl.BlockSpec((None, 1, HQ_W), lambda i: (i, 0, 0)),
            pl.BlockSpec((None, 2, A_W, buf_a), lambda i: (i, 0, 0, 0)),
            pl.BlockSpec((None, 2, B_KV_W, buf_b), lambda i: (i, 0, 0, 0)),
            pl.BlockSpec((1, buf_a), lambda i: (0, 0)),
            pl.BlockSpec((B_HEADS, 1), lambda i: (0, 0)),
        ],
        out_specs=[pl.BlockSpec((None, 1, A_W), lambda i: (i, 0, 0)),
                   pl.BlockSpec((None, 1, B_W), lambda i: (i, 0, 0))],
        out_shape=[jax.ShapeDtypeStruct((n, 1, A_W), jnp.float32),
                   jax.ShapeDtypeStruct((n, 1, B_W), jnp.float32)],
        compiler_params=pltpu.CompilerParams(
            dimension_semantics=("arbitrary",), vmem_limit_bytes=VMEM_LIMIT),
        name="sample_attn",
    )(hq_s.reshape(n, 1, HQ_W), cache_a_t, cache_b_t, _sample_weights(buf_a), sink_col)


def _out_kernel(x_ref, oa_ref, ob_ref, sg_ref, wo_ref, g_ref, b_ref, y_ref, *, alpha):
    sg = sg_ref[...].astype(jnp.float32)
    mix_a = (oa_ref[...].astype(jnp.float32) * sg[:, :A_W]).astype(jnp.bfloat16)
    mix_b = (ob_ref[...].astype(jnp.float32) * sg[:, A_W:]).astype(jnp.bfloat16)
    out = (jnp.dot(mix_a, wo_ref[:A_W, :], preferred_element_type=jnp.float32)
           + jnp.dot(mix_b, wo_ref[A_W:, :], preferred_element_type=jnp.float32))
    z = alpha * x_ref[...] + out
    mu = jnp.mean(z, axis=1, keepdims=True)
    zc = z - mu
    var = jnp.mean(zc * zc, axis=1, keepdims=True)
    y_ref[...] = zc * lax.rsqrt(var + LN_EPS) * g_ref[...] + b_ref[...]


def _output(x2, oa, ob, sg, wo, g, b, alpha, tm):
    n = x2.shape[0]
    row = lambda i: (i, 0)
    fixed = lambda i: (0, 0)
    return pl.pallas_call(
        functools.partial(_out_kernel, alpha=alpha),
        grid=(n // tm,),
        in_specs=[
            pl.BlockSpec((tm, D_MODEL), row),
            pl.BlockSpec((tm, A_W), row),
            pl.BlockSpec((tm, B_W), row),
            pl.BlockSpec((tm, G_W), row),
            pl.BlockSpec((G_W, D_MODEL), fixed),
            pl.BlockSpec((1, D_MODEL), fixed),
            pl.BlockSpec((1, D_MODEL), fixed),
        ],
        out_specs=pl.BlockSpec((tm, D_MODEL), row),
        out_shape=jax.ShapeDtypeStruct((n, D_MODEL), jnp.float32),
        compiler_params=pltpu.CompilerParams(
            dimension_semantics=("arbitrary",), vmem_limit_bytes=VMEM_LIMIT),
        name="out_proj",
    )(x2, oa, ob, sg, wo, g, b)


def _rope_tables(pos):
    inv = ROPE_THETA ** (-jnp.arange(0, ROT_DIM, 2, dtype=jnp.float32) / ROT_DIM)
    ang = pos.astype(jnp.float32)[:, None] * inv[None, :]
    cos = jnp.cos(ang)
    sin = jnp.sin(ang)
    n = pos.shape[0]
    half = ROT_DIM // 2
    pad = HEAD_DIM - ROT_DIM
    one = jnp.ones((n, pad), jnp.float32)
    zero = jnp.zeros((n, pad), jnp.float32)
    zh = jnp.zeros((n, half), jnp.float32)
    cos_h = jnp.concatenate([cos, cos, one], axis=1)
    sinp_h = jnp.concatenate([zh, sin, zero], axis=1)
    sinm_h = jnp.concatenate([-sin, zh, zero], axis=1)
    rep = LANES // HEAD_DIM
    return tuple(jnp.tile(t, (1, rep)) for t in (cos_h, sinp_h, sinm_h))


def _pair_b_heads(a, axis):
    group = B_HEADS // B_KV_HEADS
    shape = a.shape
    split = shape[:axis] + (B_KV_HEADS, group, shape[axis] // B_HEADS) + shape[axis + 1:]
    return jnp.swapaxes(a.reshape(split), axis, axis + 1).reshape(shape)


def _kv_rows(kvt, heads):
    n, _, _, rows = kvt.shape
    return jnp.transpose(kvt.reshape(n, 2, heads, HEAD_DIM, rows), (0, 4, 1, 2, 3))


def _layer(xp, xs, cache_a, cache_b, w_in, sinks, w_o, ln_g, ln_b, alpha, past_len):
    b, seq, _ = xp.shape
    nb = xs.shape[0]
    qa, ka, va, ga, qb, kb, vb, gb = jnp.split(
        w_in, np.cumsum([A_W, A_W, A_W, A_W, B_W, B_KV_W, B_KV_W])[:].tolist(), axis=1)
    q_scale = HEAD_DIM ** -0.5
    w = jnp.concatenate([qa * q_scale, ka, va, _pair_b_heads(qb, 1) * q_scale, kb, vb,
                         ga, _pair_b_heads(gb, 1)], axis=1).astype(jnp.bfloat16)
    wo = jnp.concatenate([w_o[:A_W], _pair_b_heads(w_o[A_W:], 0)], axis=0).astype(jnp.bfloat16)
    sink_p = _pair_b_heads(sinks.astype(jnp.float32), 0)
    sink_tiles = jnp.repeat(sink_p, HEAD_DIM).reshape(B_W // LANES, 1, LANES)
    g2 = ln_g.reshape(1, D_MODEL)
    b2 = ln_b.reshape(1, D_MODEL)

    tm = 512
    wa = min(A_WINDOW, seq)
    wb = min(B_WINDOW, seq)
    x2 = xp.reshape(b * seq, D_MODEL)
    hq, sg, kva_t, kvb_t = _project(x2, w, *_rope_tables(jnp.arange(seq)), tm, seq, wa, wb)
    hq3 = hq.reshape(b, seq, HQ_W)
    oa = _attention(hq3, QA_T, KA_T, VA_T, False, tuple(sorted(DILATIONS, reverse=True)))
    ob = _attention(hq3, QB_T, KB_T, VB_T, True, (1,), sink=sink_tiles)
    y = _output(x2, oa.reshape(b * seq, A_W), ob.reshape(b * seq, B_W), sg, wo, g2, b2, alpha, tm)
    kv_a = _kv_rows(kva_t, A_HEADS)
    kv_b = _kv_rows(kvb_t, B_KV_HEADS)

    xs2 = xs.reshape(nb, D_MODEL)
    pos_s = jnp.full((nb,), past_len, jnp.int32)
    hq_s, sg_s, new_a_t, new_b_t = _project(xs2, w, *_rope_tables(pos_s), nb, nb, nb, nb)
    ca_t = jnp.transpose(cache_a, (0, 2, 3, 4, 1)).reshape(nb, 2, A_W, cache_a.shape[1])
    cb_t = jnp.transpose(cache_b, (0, 2, 3, 4, 1)).reshape(nb, 2, B_KV_W, cache_b.shape[1])
    oa_s, ob_s = _sample_attention(hq_s, ca_t, cb_t, sink_p.reshape(B_HEADS, 1))
    y_s = _output(xs2, oa_s.reshape(nb, A_W), ob_s.reshape(nb, B_W), sg_s, wo, g2, b2, alpha, nb)
    new_a = _kv_rows(new_a_t, A_HEADS).reshape(nb, 1, 2, A_HEADS, HEAD_DIM)
    new_b = _kv_rows(new_b_t, B_KV_HEADS).reshape(nb, 1, 2, B_KV_HEADS, HEAD_DIM)
    return (y.reshape(b, seq, D_MODEL), y_s.reshape(nb, 1, D_MODEL), kv_a, kv_b, new_a, new_b)


def kernel(x_prompt, x_sample, cache_a_kv, cache_b_kv, w_in, attn_sinks, w_o, ln_g, ln_b):
    depth = w_in.shape[0]
    assert depth == 1 and x_sample.shape[1] == 1, "single layer, one sample token per sequence"
    assert cache_a_kv.shape[2] == BAND * max(DILATIONS), "mixer-A cache covers every strided read"
    assert cache_b_kv.shape[2] == B_WINDOW
    alpha = (2 * depth) ** 0.25
    outs = _layer(x_prompt, x_sample, cache_a_kv[0], cache_b_kv[0], w_in[0], attn_sinks[0],
                  w_o[0], ln_g[0], ln_b[0], alpha, PAST_LEN)
    yp, ys, kv_a, kv_b, new_a, new_b = outs
    return (yp, ys, kv_a[None], kv_b[None], new_a[None], new_b[None])
```

```python
import functools

import jax
import jax.numpy as jnp
import numpy as np
from jax import lax
from jax.experimental import pallas as pl
from jax.experimental.pallas import tpu as pltpu

D_MODEL = 1024
HEAD_DIM = 64
A_HEADS = 8
B_HEADS = 8
B_KV_HEADS = 2
DILATIONS = (1, 4, 16)
BAND = 128
A_WINDOW = 2048
B_WINDOW = 128
PAST_LEN = 16384
ROT_DIM = HEAD_DIM // 4
ROPE_THETA = 500000.0
LN_EPS = 1e-5
A_W = A_HEADS * HEAD_DIM
B_W = B_HEADS * HEAD_DIM
B_KV_W = B_KV_HEADS * HEAD_DIM
LANES = 128
HQ_W = 3 * A_W + B_W + 2 * B_KV_W
G_W = A_W + B_W
IN_WIDTH = HQ_W + G_W
QA_T, KA_T, VA_T = 0, A_W // LANES, 2 * A_W // LANES
QB_T = 3 * A_W // LANES
KB_T = QB_T + B_W // LANES
VB_T = KB_T + 1
VMEM_LIMIT = 56 * 1024 * 1024
COL_CHUNK = 256
NEG_INF = float("-inf")
LOG2E = 1.4426950408889634
BLOCKS_IN_FLIGHT = 8


def _proj_kernel(x_ref, w_ref, cos_ref, sinp_ref, sinm_ref, hq_ref, sg_ref, kva_ref, kvb_ref, *, wb):
    tm = x_ref.shape[0]
    x = x_ref[...].astype(jnp.bfloat16)
    cos = cos_ref[...]
    sinp = sinp_ref[...]
    sinm = sinm_ref[...]
    rope_tiles = set(range(QA_T, VA_T)) | set(range(QB_T, VB_T))
    for c in range(IN_WIDTH // COL_CHUNK):
        acc = jnp.dot(x, w_ref[:, c * COL_CHUNK:(c + 1) * COL_CHUNK],
                      preferred_element_type=jnp.float32)
        for half in range(COL_CHUNK // LANES):
            tile = c * (COL_CHUNK // LANES) + half
            t = acc[:, half * LANES:(half + 1) * LANES]
            if tile < HQ_W // LANES:
                if tile in rope_tiles:
                    t = (t * cos + pltpu.roll(t, ROT_DIM // 2, 1) * sinp
                         + pltpu.roll(t, LANES - ROT_DIM // 2, 1) * sinm)
                hq_ref[:, tile * LANES:(tile + 1) * LANES] = t
                if KA_T <= tile < QB_T:
                    kv, j = divmod(tile - KA_T, A_W // LANES)
                    kva_ref[kv, j * LANES:(j + 1) * LANES, :] = t.T
                elif tile >= KB_T:
                    kvb_ref[tile - KB_T] = t[tm - wb:].T
            else:
                g = tile - HQ_W // LANES
                sg_ref[:, g * LANES:(g + 1) * LANES] = (t * jax.nn.sigmoid(t)).astype(sg_ref.dtype)


def _project(x2, w, cos, sinp, sinm, tm, seq, wa, wb):
    n = x2.shape[0]
    tab_blocks = cos.shape[0] // tm
    tps = seq // tm
    first_a = tps - wa // tm
    assert seq % tm == 0 and wa % tm == 0 and wb <= tm and cos.shape[0] == seq
    row = lambda i: (i, 0)
    tab = lambda i: (i % tab_blocks, 0)
    return pl.pallas_call(
        functools.partial(_proj_kernel, wb=wb),
        grid=(n // tm,),
        in_specs=[
            pl.BlockSpec((tm, D_MODEL), row),
            pl.BlockSpec((D_MODEL, IN_WIDTH), lambda i: (0, 0)),
            pl.BlockSpec((tm, LANES), tab),
            pl.BlockSpec((tm, LANES), tab),
            pl.BlockSpec((tm, LANES), tab),
        ],
        out_specs=[
            pl.BlockSpec((tm, HQ_W), row),
            pl.BlockSpec((tm, G_W), row),
            pl.BlockSpec((None, 2, A_W, tm),
                         lambda i: (i // tps, 0, 0, jnp.maximum(i % tps - first_a, 0))),
            pl.BlockSpec((None, 2, B_KV_W, wb), lambda i: (i // tps, 0, 0, 0)),
        ],
        out_shape=[jax.ShapeDtypeStruct((n, HQ_W), jnp.float32),
                   jax.ShapeDtypeStruct((n, G_W), jnp.bfloat16),
                   jax.ShapeDtypeStruct((n // seq, 2, A_W, wa), jnp.float32),
                   jax.ShapeDtypeStruct((n // seq, 2, B_KV_W, wb), jnp.float32)],
        compiler_params=pltpu.CompilerParams(
            dimension_semantics=("arbitrary",), vmem_limit_bytes=VMEM_LIMIT),
        name="proj",
    )(x2, w, cos, sinp, sinm)


def _lane_lo(shape):
    return lax.broadcasted_iota(jnp.int32, shape, len(shape) - 1) < HEAD_DIM


def _attn_kernel(*refs, dilations, seq, with_sink):
    if with_sink:
        q_ref, k_ref, v_ref, bias_ref, sink_ref, o_ref = refs[:6]
        scratch = refs[6:]
    else:
        q_ref, k_ref, v_ref, bias_ref, o_ref = refs[:5]
        scratch = refs[5:]
    multi = len(dilations) > 1
    if multi:
        acc_ref, m_ref, l_ref = scratch
    lo = _lane_lo((BAND, LANES))
    ones = jnp.ones((2 * BAND, LANES), jnp.bfloat16)

    def pair(a, b):
        return jnp.where(lo, a, b)

    for ci, d in enumerate(dilations):
        first = ci == 0
        last = ci == len(dilations) - 1
        nblk = seq // d // BAND

        def rows(start, d=d):
            return pl.ds(start, BAND) if d == 1 else pl.ds(start, BAND, stride=d)

        def block(i, carry, r, d=d, first=first, last=last, rows=rows):
            kp, vp = carry
            start = r + i * (BAND * d)
            q = q_ref[rows(start), :]
            kc = k_ref[rows(start), :].astype(jnp.bfloat16)
            vc = v_ref[rows(start), :].astype(jnp.bfloat16)
            q2 = jnp.concatenate([jnp.where(lo, q, 0.0), jnp.where(lo, 0.0, q)],
                                 axis=0).astype(jnp.bfloat16)
            kk = jnp.concatenate([kp, kc], axis=0)
            s = lax.dot_general(q2, kk, (((1,), (1,)), ((), ())),
                                preferred_element_type=jnp.float32)
            s = s + bias_ref[jnp.minimum(i, 1)]
            m = jnp.max(s, axis=1, keepdims=True)
            p = jnp.exp2(s - m).astype(jnp.bfloat16)
            vv = jnp.concatenate([jnp.concatenate([vp, vc], axis=0), ones], axis=1)
            pv = jnp.dot(p, vv, preferred_element_type=jnp.float32)
            o_n = pair(pv[:BAND, :LANES], pv[BAND:, :LANES])
            l_n = pair(pv[:BAND, LANES:], pv[BAND:, LANES:])
            m_n = pair(jnp.broadcast_to(m[:BAND], (BAND, LANES)),
                       jnp.broadcast_to(m[BAND:], (BAND, LANES)))
            if multi and not first:
                m_o = m_ref[rows(start), :]
                m_t = jnp.maximum(m_o, m_n)
                a_o = jnp.exp2(m_o - m_t)
                a_n = jnp.exp2(m_n - m_t)
                o_n = acc_ref[rows(start), :] * a_o + o_n * a_n
                l_n = l_ref[rows(start), :] * a_o + l_n * a_n
                m_n = m_t
            if last:
                if with_sink:
                    sink = sink_ref[...]
                    m_t = jnp.maximum(m_n, sink)
                    a_n = jnp.exp2(m_n - m_t)
                    o_n = o_n * a_n
                    l_n = l_n * a_n + jnp.exp2(sink - m_t)
                o_ref[rows(start), :] = (o_n / l_n).astype(o_ref.dtype)
            else:
                acc_ref[rows(start), :] = o_n
                l_ref[rows(start), :] = l_n
                m_ref[rows(start), :] = m_n
            return kc, vc

        zero = jnp.zeros((BAND, LANES), jnp.bfloat16)

        inner = min(nblk, BLOCKS_IN_FLIGHT)

        def residue(r, _, nblk=nblk, block=block, inner=inner):
            lax.fori_loop(0, nblk, functools.partial(block, r=r), (zero, zero), unroll=inner)
            return 0

        if d == 1:
            residue(0, 0)
        else:
            lax.fori_loop(0, d, residue, 0, unroll=BLOCKS_IN_FLIGHT // inner)


def _band_bias():
    qi = np.arange(BAND)[:, None]
    ki = np.arange(2 * BAND)[None, :]
    dist = BAND + qi - ki
    band = (dist >= 0) & (dist <= BAND)
    full = np.where(band, 0.0, NEG_INF).astype(np.float32)
    head = np.where(band & (ki >= BAND), 0.0, NEG_INF).astype(np.float32)
    both = np.stack([head, full])
    return jnp.asarray(np.concatenate([both, both], axis=1))


def _attention(hq3, q_tile, k_tile, v_tile, shared_kv, dilations, sink=None):
    b, seq, _ = hq3.shape
    n_pairs = A_W // LANES
    with_sink = sink is not None
    kv_off = (lambda u: 0) if shared_kv else (lambda u: u)
    in_specs = [
        pl.BlockSpec((None, seq, LANES), lambda i, u: (i, 0, q_tile + u)),
        pl.BlockSpec((None, seq, LANES), lambda i, u: (i, 0, k_tile + kv_off(u))),
        pl.BlockSpec((None, seq, LANES), lambda i, u: (i, 0, v_tile + kv_off(u))),
        pl.BlockSpec((2, 2 * BAND, 2 * BAND), lambda i, u: (0, 0, 0)),
    ]
    args = [hq3, hq3, hq3, _band_bias()]
    if with_sink:
        in_specs.append(pl.BlockSpec((None, 1, LANES), lambda i, u: (u, 0, 0)))
        args.append(sink)
    scratch = []
    if len(dilations) > 1:
        scratch = [pltpu.VMEM((seq, LANES), jnp.float32)] * 3
    return pl.pallas_call(
        functools.partial(_attn_kernel, dilations=dilations, seq=seq, with_sink=with_sink),
        grid=(b, n_pairs),
        in_specs=in_specs,
        out_specs=pl.BlockSpec((None, seq, LANES), lambda i, u: (i, 0, u)),
        out_shape=jax.ShapeDtypeStruct((b, seq, n_pairs * LANES), jnp.bfloat16),
        scratch_shapes=scratch,
        compiler_params=pltpu.CompilerParams(
            dimension_semantics=("arbitrary", "arbitrary"), vmem_limit_bytes=VMEM_LIMIT),
        name="attn_b" if with_sink else "attn_a",
    )(*args)


def _sample_kernel(hq_ref, ca_ref, cb_ref, w_ref, sink_ref, oa_ref, ob_ref):
    row = hq_ref[...]
    qa = row[:, QA_T * LANES:KA_T * LANES]
    ka_new = row[:, KA_T * LANES:VA_T * LANES]
    va_new = row[:, VA_T * LANES:QB_T * LANES]
    qb = row[:, QB_T * LANES:KB_T * LANES]
    kb_new = row[:, KB_T * LANES:VB_T * LANES]
    vb_new = row[:, VB_T * LANES:HQ_W]

    hrow = lax.broadcasted_iota(jnp.int32, (A_HEADS, A_W), 0)
    hcol = lax.broadcasted_iota(jnp.int32, (A_HEADS, A_W), 1) // HEAD_DIM
    own = hrow == hcol
    q_bd = jnp.where(own, jnp.broadcast_to(qa, (A_HEADS, A_W)), 0.0)
    s = jnp.dot(q_bd.astype(jnp.bfloat16), ca_ref[0].astype(jnp.bfloat16),
                preferred_element_type=jnp.float32)
    s_new = jnp.sum(q_bd * ka_new, axis=1, keepdims=True)
    w = w_ref[...]
    s = jnp.where(w > 0.0, s, NEG_INF)
    m = jnp.maximum(jnp.max(s, axis=1, keepdims=True), s_new)
    p = jnp.exp2(s - m) * w
    p_new = float(len(DILATIONS)) * jnp.exp2(s_new - m)
    den = jnp.sum(p, axis=1, keepdims=True) + p_new
    o = lax.dot_general(p.astype(jnp.bfloat16), ca_ref[1].astype(jnp.bfloat16),
                        (((1,), (1,)), ((), ())), preferred_element_type=jnp.float32)
    o = (o + p_new * va_new) / den
    oa_ref[...] = jnp.sum(jnp.where(own, o, 0.0), axis=0, keepdims=True)

    prow = lax.broadcasted_iota(jnp.int32, (B_HEADS, LANES), 0)
    pcol = lax.broadcasted_iota(jnp.int32, (B_HEADS, LANES), 1) // HEAD_DIM
    qb_rows = jnp.zeros((B_HEADS, LANES), jnp.float32)
    for t in range(B_W // LANES):
        qb_rows = jnp.where(prow // 2 == t,
                            jnp.broadcast_to(qb[:, t * LANES:(t + 1) * LANES], (B_HEADS, LANES)), qb_rows)
    own_b = (prow % 2) == pcol
    qb_bd = jnp.where(own_b, qb_rows, 0.0)
    sb = jnp.dot(qb_bd.astype(jnp.bfloat16), cb_ref[0].astype(jnp.bfloat16),
                 preferred_element_type=jnp.float32)
    sb_new = jnp.sum(qb_bd * kb_new, axis=1, keepdims=True)
    sink = sink_ref[...]
    mb = jnp.maximum(jnp.maximum(jnp.max(sb, axis=1, keepdims=True), sb_new), sink)
    pb = jnp.exp2(sb - mb)
    pb_new = jnp.exp2(sb_new - mb)
    den_b = jnp.sum(pb, axis=1, keepdims=True) + pb_new + jnp.exp2(sink - mb)
    ob = lax.dot_general(pb.astype(jnp.bfloat16), cb_ref[1].astype(jnp.bfloat16),
                         (((1,), (1,)), ((), ())), preferred_element_type=jnp.float32)
    ob = (ob + pb_new * vb_new) / den_b
    lo = _lane_lo((1, LANES))
    for t in range(B_W // LANES):
        ob_ref[:, t * LANES:(t + 1) * LANES] = jnp.where(lo, ob[2 * t:2 * t + 1], ob[2 * t + 1:2 * t + 2])


def _sample_weights(buf):
    dist = buf - np.arange(buf)
    w = np.zeros((buf,), np.float32)
    for d in DILATIONS:
        w += ((dist % d == 0) & (dist <= BAND * d)).astype(np.float32)
    return jnp.asarray(w[None, :])


def _sample_attention(hq_s, cache_a_t, cache_b_t, sink_col):
    n = hq_s.shape[0]
    buf_a = cache_a_t.shape[-1]
    buf_b = cache_b_t.shape[-1]
    return pl.pallas_call(
        _sample_kernel,
        grid=(n,),
        in_specs=[
            pl.BlockSpec((None, 1, HQ_W), lambda i: (i, 0, 0)),
            pl.BlockSpec((None, 2, A_W, buf_a), lambda i: (i, 0, 0, 0)),
            pl.BlockSpec((None, 2, B_KV_W, buf_b), lambda i: (i, 0, 0, 0)),
            pl.BlockSpec((1, buf_a), lambda i: (0, 0)),
            pl.BlockSpec((B_HEADS, 1), lambda i: (0, 0)),
        ],
        out_specs=[pl.BlockSpec((None, 1, A_W), lambda i: (i, 0, 0)),
                   pl.BlockSpec((None, 1, B_W), lambda i: (i, 0, 0))],
        out_shape=[jax.ShapeDtypeStruct((n, 1, A_W), jnp.float32),
                   jax.ShapeDtypeStruct((n, 1, B_W), jnp.float32)],
        compiler_params=pltpu.CompilerParams(
            dimension_semantics=("arbitrary",), vmem_limit_bytes=VMEM_LIMIT),
        name="sample_attn",
    )(hq_s.reshape(n, 1, HQ_W), cache_a_t, cache_b_t, _sample_weights(buf_a), sink_col)


def _out_kernel(x_ref, oa_ref, ob_ref, sg_ref, wo_ref, g_ref, b_ref, y_ref, *, alpha):
    sg = sg_ref[...].astype(jnp.float32)
    mix_a = (oa_ref[...].astype(jnp.float32) * sg[:, :A_W]).astype(jnp.bfloat16)
    mix_b = (ob_ref[...].astype(jnp.float32) * sg[:, A_W:]).astype(jnp.bfloat16)
    out = (jnp.dot(mix_a, wo_ref[:A_W, :], preferred_element_type=jnp.float32)
           + jnp.dot(mix_b, wo_ref[A_W:, :], preferred_element_type=jnp.float32))
    z = alpha * x_ref[...] + out
    mu = jnp.mean(z, axis=1, keepdims=True)
    zc = z - mu
    var = jnp.mean(zc * zc, axis=1, keepdims=True)
    y_ref[...] = zc * lax.rsqrt(var + LN_EPS) * g_ref[...] + b_ref[...]


def _output(x2, oa, ob, sg, wo, g, b, alpha, tm):
    n = x2.shape[0]
    row = lambda i: (i, 0)
    fixed = lambda i: (0, 0)
    return pl.pallas_call(
        functools.partial(_out_kernel, alpha=alpha),
        grid=(n // tm,),
        in_specs=[
            pl.BlockSpec((tm, D_MODEL), row),
            pl.BlockSpec((tm, A_W), row),
            pl.BlockSpec((tm, B_W), row),
            pl.BlockSpec((tm, G_W), row),
            pl.BlockSpec((G_W, D_MODEL), fixed),
            pl.BlockSpec((1, D_MODEL), fixed),
            pl.BlockSpec((1, D_MODEL), fixed),
        ],
        out_specs=pl.BlockSpec((tm, D_MODEL), row),
        out_shape=jax.ShapeDtypeStruct((n, D_MODEL), jnp.float32),
        compiler_params=pltpu.CompilerParams(
            dimension_semantics=("arbitrary",), vmem_limit_bytes=VMEM_LIMIT),
        name="out_proj",
    )(x2, oa, ob, sg, wo, g, b)


def _rope_tables(pos):
    inv = ROPE_THETA ** (-jnp.arange(0, ROT_DIM, 2, dtype=jnp.float32) / ROT_DIM)
    ang = pos.astype(jnp.float32)[:, None] * inv[None, :]
    cos = jnp.cos(ang)
    sin = jnp.sin(ang)
    n = pos.shape[0]
    half = ROT_DIM // 2
    pad = HEAD_DIM - ROT_DIM
    one = jnp.ones((n, pad), jnp.float32)
    zero = jnp.zeros((n, pad), jnp.float32)
    zh = jnp.zeros((n, half), jnp.float32)
    cos_h = jnp.concatenate([cos, cos, one], axis=1)
    sinp_h = jnp.concatenate([zh, sin, zero], axis=1)
    sinm_h = jnp.concatenate([-sin, zh, zero], axis=1)
    rep = LANES // HEAD_DIM
    return tuple(jnp.tile(t, (1, rep)) for t in (cos_h, sinp_h, sinm_h))


def _pair_b_heads(a, axis):
    group = B_HEADS // B_KV_HEADS
    shape = a.shape
    split = shape[:axis] + (B_KV_HEADS, group, shape[axis] // B_HEADS) + shape[axis + 1:]
    return jnp.swapaxes(a.reshape(split), axis, axis + 1).reshape(shape)


def _kv_rows(kvt, heads):
    n, _, _, rows = kvt.shape
    return jnp.transpose(kvt.reshape(n, 2, heads, HEAD_DIM, rows), (0, 4, 1, 2, 3))


def _layer(xp, xs, cache_a, cache_b, w_in, sinks, w_o, ln_g, ln_b, alpha, past_len):
    b, seq, _ = xp.shape
    nb = xs.shape[0]
    qa, ka, va, ga, qb, kb, vb, gb = jnp.split(
        w_in, np.cumsum([A_W, A_W, A_W, A_W, B_W, B_KV_W, B_KV_W])[:].tolist(), axis=1)
    q_scale = HEAD_DIM ** -0.5 * LOG2E
    w = jnp.concatenate([qa * q_scale, ka, va, _pair_b_heads(qb, 1) * q_scale, kb, vb,
                         ga, _pair_b_heads(gb, 1)], axis=1).astype(jnp.bfloat16)
    wo = jnp.concatenate([w_o[:A_W], _pair_b_heads(w_o[A_W:], 0)], axis=0).astype(jnp.bfloat16)
    sink_p = _pair_b_heads(sinks.astype(jnp.float32) * LOG2E, 0)
    sink_tiles = jnp.repeat(sink_p, HEAD_DIM).reshape(B_W // LANES, 1, LANES)
    g2 = ln_g.reshape(1, D_MODEL)
    b2 = ln_b.reshape(1, D_MODEL)

    tm = 512
    wa = min(A_WINDOW, seq)
    wb = min(B_WINDOW, seq)
    x2 = xp.reshape(b * seq, D_MODEL)
    hq, sg, kva_t, kvb_t = _project(x2, w, *_rope_tables(jnp.arange(seq)), tm, seq, wa, wb)
    hq3 = hq.reshape(b, seq, HQ_W)
    oa = _attention(hq3, QA_T, KA_T, VA_T, False, tuple(sorted(DILATIONS, reverse=True)))
    ob = _attention(hq3, QB_T, KB_T, VB_T, True, (1,), sink=sink_tiles)
    y = _output(x2, oa.reshape(b * seq, A_W), ob.reshape(b * seq, B_W), sg, wo, g2, b2, alpha, 2 * tm)
    kv_a = _kv_rows(kva_t, A_HEADS)
    kv_b = _kv_rows(kvb_t, B_KV_HEADS)

    xs2 = xs.reshape(nb, D_MODEL)
    pos_s = jnp.full((nb,), past_len, jnp.int32)
    hq_s, sg_s, new_a_t, new_b_t = _project(xs2, w, *_rope_tables(pos_s), nb, nb, nb, nb)
    ca_t = jnp.transpose(cache_a, (0, 2, 3, 4, 1)).reshape(nb, 2, A_W, cache_a.shape[1])
    cb_t = jnp.transpose(cache_b, (0, 2, 3, 4, 1)).reshape(nb, 2, B_KV_W, cache_b.shape[1])
    oa_s, ob_s = _sample_attention(hq_s, ca_t, cb_t, sink_p.reshape(B_HEADS, 1))
    y_s = _output(xs2, oa_s.reshape(nb, A_W), ob_s.reshape(nb, B_W), sg_s, wo, g2, b2, alpha, nb)
    new_a = _kv_rows(new_a_t, A_HEADS).reshape(nb, 1, 2, A_HEADS, HEAD_DIM)
    new_b = _kv_rows(new_b_t, B_KV_HEADS).reshape(nb, 1, 2, B_KV_HEADS, HEAD_DIM)
    return (y.reshape(b, seq, D_MODEL), y_s.reshape(nb, 1, D_MODEL), kv_a, kv_b, new_a, new_b)


def kernel(x_prompt, x_sample, cache_a_kv, cache_b_kv, w_in, attn_sinks, w_o, ln_g, ln_b):
    depth = w_in.shape[0]
    assert depth == 1 and x_sample.shape[1] == 1, "single layer, one sample token per sequence"
    assert cache_a_kv.shape[2] == BAND * max(DILATIONS), "mixer-A cache covers every strided read"
    assert cache_b_kv.shape[2] == B_WINDOW
    alpha = (2 * depth) ** 0.25
    outs = _layer(x_prompt, x_sample, cache_a_kv[0], cache_b_kv[0], w_in[0], attn_sinks[0],
                  w_o[0], ln_g[0], ln_b[0], alpha, PAST_LEN)
    yp, ys, kv_a, kv_b, new_a, new_b = outs
    return (yp, ys, kv_a[None], kv_b[None], new_a[None], new_b[None])
```

```python
import functools

import jax
import jax.numpy as jnp
import numpy as np
from jax import lax
from jax.experimental import pallas as pl
from jax.experimental.pallas import tpu as pltpu

D_MODEL = 1024
HEAD_DIM = 64
A_HEADS = 8
B_HEADS = 8
B_KV_HEADS = 2
DILATIONS = (1, 4, 16)
BAND = 128
A_WINDOW = 2048
B_WINDOW = 128
PAST_LEN = 16384
ROT_DIM = HEAD_DIM // 4
ROPE_THETA = 500000.0
LN_EPS = 1e-5
A_W = A_HEADS * HEAD_DIM
B_W = B_HEADS * HEAD_DIM
B_KV_W = B_KV_HEADS * HEAD_DIM
LANES = 128
HQ_W = 3 * A_W + B_W + 2 * B_KV_W
G_W = A_W + B_W
IN_WIDTH = HQ_W + G_W
QA_T, KA_T, VA_T = 0, A_W // LANES, 2 * A_W // LANES
QB_T = 3 * A_W // LANES
KB_T = QB_T + B_W // LANES
VB_T = KB_T + 1
VMEM_LIMIT = 56 * 1024 * 1024
COL_CHUNK = 256
NEG_INF = float("-inf")
LOG2E = 1.4426950408889634
BLOCKS_IN_FLIGHT = 8


def _proj_kernel(x_ref, w_ref, cos_ref, sinp_ref, sinm_ref, hq_ref, sg_ref, kva_ref, kvb_ref, *, wb):
    tm = x_ref.shape[0]
    x = x_ref[...].astype(jnp.bfloat16)
    cos = cos_ref[...]
    sinp = sinp_ref[...]
    sinm = sinm_ref[...]
    rope_tiles = set(range(QA_T, VA_T)) | set(range(QB_T, VB_T))
    for c in range(IN_WIDTH // COL_CHUNK):
        acc = jnp.dot(x, w_ref[:, c * COL_CHUNK:(c + 1) * COL_CHUNK],
                      preferred_element_type=jnp.float32)
        for half in range(COL_CHUNK // LANES):
            tile = c * (COL_CHUNK // LANES) + half
            t = acc[:, half * LANES:(half + 1) * LANES]
            if tile < HQ_W // LANES:
                if tile in rope_tiles:
                    t = (t * cos + pltpu.roll(t, ROT_DIM // 2, 1) * sinp
                         + pltpu.roll(t, LANES - ROT_DIM // 2, 1) * sinm)
                hq_ref[:, tile * LANES:(tile + 1) * LANES] = t
                if KA_T <= tile < QB_T:
                    kv, j = divmod(tile - KA_T, A_W // LANES)
                    kva_ref[kv, j * LANES:(j + 1) * LANES, :] = t.T
                elif tile >= KB_T:
                    kvb_ref[tile - KB_T] = t[tm - wb:].T
            else:
                g = tile - HQ_W // LANES
                sg_ref[:, g * LANES:(g + 1) * LANES] = (t * jax.nn.sigmoid(t)).astype(sg_ref.dtype)


def _project(x2, w, cos, sinp, sinm, tm, seq, wa, wb):
    n = x2.shape[0]
    tab_blocks = cos.shape[0] // tm
    tps = seq // tm
    first_a = tps - wa // tm
    assert seq % tm == 0 and wa % tm == 0 and wb <= tm and cos.shape[0] == seq
    row = lambda i: (i, 0)
    tab = lambda i: (i % tab_blocks, 0)
    return pl.pallas_call(
        functools.partial(_proj_kernel, wb=wb),
        grid=(n // tm,),
        in_specs=[
            pl.BlockSpec((tm, D_MODEL), row),
            pl.BlockSpec((D_MODEL, IN_WIDTH), lambda i: (0, 0)),
            pl.BlockSpec((tm, LANES), tab),
            pl.BlockSpec((tm, LANES), tab),
            pl.BlockSpec((tm, LANES), tab),
        ],
        out_specs=[
            pl.BlockSpec((tm, HQ_W), row),
            pl.BlockSpec((tm, G_W), row),
            pl.BlockSpec((None, 2, A_W, tm),
                         lambda i: (i // tps, 0, 0, jnp.maximum(i % tps - first_a, 0))),
            pl.BlockSpec((None, 2, B_KV_W, wb), lambda i: (i // tps, 0, 0, 0)),
        ],
        out_shape=[jax.ShapeDtypeStruct((n, HQ_W), jnp.float32),
                   jax.ShapeDtypeStruct((n, G_W), jnp.bfloat16),
                   jax.ShapeDtypeStruct((n // seq, 2, A_W, wa), jnp.float32),
                   jax.ShapeDtypeStruct((n // seq, 2, B_KV_W, wb), jnp.float32)],
        compiler_params=pltpu.CompilerParams(
            dimension_semantics=("arbitrary",), vmem_limit_bytes=VMEM_LIMIT),
        name="proj",
    )(x2, w, cos, sinp, sinm)


def _lane_lo(shape):
    return lax.broadcasted_iota(jnp.int32, shape, len(shape) - 1) < HEAD_DIM


def _sample_step(hq_ref, ca_ref, cb_ref, w_ref, sink_ref, oa_ref, ob_ref):
    row = hq_ref[...]
    qa = row[:, QA_T * LANES:KA_T * LANES]
    ka_new = row[:, KA_T * LANES:VA_T * LANES]
    va_new = row[:, VA_T * LANES:QB_T * LANES]
    qb = row[:, QB_T * LANES:KB_T * LANES]
    kb_new = row[:, KB_T * LANES:VB_T * LANES]
    vb_new = row[:, VB_T * LANES:HQ_W]
    buf_a = ca_ref.shape[-1]

    hrow = lax.broadcasted_iota(jnp.int32, (A_HEADS, A_W), 0)
    hcol = lax.broadcasted_iota(jnp.int32, (A_HEADS, A_W), 1) // HEAD_DIM
    own = hrow == hcol

    def spread(col):
        return jnp.sum(jnp.where(own, jnp.broadcast_to(col, (A_HEADS, A_W)), 0.0), axis=0, keepdims=True)

    q_bd = jnp.where(own, jnp.broadcast_to(qa, (A_HEADS, A_W)), 0.0)
    s = jnp.dot(q_bd.astype(jnp.bfloat16), ca_ref[0].astype(jnp.bfloat16),
                preferred_element_type=jnp.float32)
    s_new = jnp.sum(q_bd * ka_new, axis=1, keepdims=True)
    w = w_ref[...]
    s = jnp.where(w > 0.0, s, NEG_INF)
    m = jnp.maximum(jnp.max(s, axis=1, keepdims=True), s_new)
    p = jnp.exp2(s - m) * w
    p_new = float(len(DILATIONS)) * jnp.exp2(s_new - m)
    den = jnp.sum(p, axis=1, keepdims=True) + p_new
    folded = []
    for h in range(A_HEADS):
        acc = None
        for t in range(buf_a // LANES):
            cols = slice(t * LANES, (t + 1) * LANES)
            part = ca_ref[1, h * HEAD_DIM:(h + 1) * HEAD_DIM, cols] * p[h:h + 1, cols]
            acc = part if acc is None else acc + part
        folded.append(acc)
    o = jnp.sum(jnp.concatenate(folded, axis=0).T, axis=0, keepdims=True)
    oa_ref[...] = (o + spread(p_new) * va_new) / spread(den)

    prow = lax.broadcasted_iota(jnp.int32, (B_HEADS, LANES), 0)
    pcol = lax.broadcasted_iota(jnp.int32, (B_HEADS, LANES), 1) // HEAD_DIM
    qb_rows = jnp.zeros((B_HEADS, LANES), jnp.float32)
    for t in range(B_W // LANES):
        qb_rows = jnp.where(prow // 2 == t,
                            jnp.broadcast_to(qb[:, t * LANES:(t + 1) * LANES], (B_HEADS, LANES)), qb_rows)
    own_b = (prow % 2) == pcol
    qb_bd = jnp.where(own_b, qb_rows, 0.0)
    sb = jnp.dot(qb_bd.astype(jnp.bfloat16), cb_ref[0].astype(jnp.bfloat16),
                 preferred_element_type=jnp.float32)
    sb_new = jnp.sum(qb_bd * kb_new, axis=1, keepdims=True)
    sink = sink_ref[...]
    mb = jnp.maximum(jnp.maximum(jnp.max(sb, axis=1, keepdims=True), sb_new), sink)
    pb = jnp.exp2(sb - mb)
    pb_new = jnp.exp2(sb_new - mb)
    den_b = jnp.sum(pb, axis=1, keepdims=True) + pb_new + jnp.exp2(sink - mb)
    ob = lax.dot_general(pb.astype(jnp.bfloat16), cb_ref[1].astype(jnp.bfloat16),
                         (((1,), (1,)), ((), ())), preferred_element_type=jnp.float32)
    ob = (ob + pb_new * vb_new) / den_b
    lo = _lane_lo((1, LANES))
    for t in range(B_W // LANES):
        ob_ref[:, t * LANES:(t + 1) * LANES] = jnp.where(lo, ob[2 * t:2 * t + 1], ob[2 * t + 1:2 * t + 2])


def _sample_weights(buf):
    dist = buf - np.arange(buf)
    w = np.zeros((buf,), np.float32)
    for d in DILATIONS:
        w += ((dist % d == 0) & (dist <= BAND * d)).astype(np.float32)
    return jnp.asarray(w[None, :])


def _attn_kernel(*refs, phases, seq, with_sink, with_sample):
    refs = list(refs)
    q_ref, k_ref, v_ref, bias_ref = refs[:4]
    del refs[:4]
    sink_ref = refs.pop(0) if with_sink else None
    sample_in = [refs.pop(0) for _ in range(5)] if with_sample else None
    o_ref = refs.pop(0)
    sample_out = [refs.pop(0) for _ in range(2)] if with_sample else None
    dils = [d for d, _, _ in phases]
    multi = len(set(dils)) > 1
    if multi:
        acc_ref, m_ref, l_ref = refs
    lo = _lane_lo((BAND, LANES))
    ones = jnp.ones((2 * BAND, LANES), jnp.bfloat16)

    def pair(a, b):
        return jnp.where(lo, a, b)

    if with_sample:
        _sample_step(*sample_in, *sample_out)

    def run(d, i_lo, i_hi):
        first = d == dils[0]
        last = d == dils[-1]

        def rows(start):
            return pl.ds(start, BAND) if d == 1 else pl.ds(start, BAND, stride=d)

        def block(i, carry, r):
            kp, vp = carry
            start = r + i * (BAND * d)
            q = q_ref[rows(start), :]
            kc = k_ref[rows(start), :].astype(jnp.bfloat16)
            vc = v_ref[rows(start), :].astype(jnp.bfloat16)
            q2 = jnp.concatenate([jnp.where(lo, q, 0.0), jnp.where(lo, 0.0, q)],
                                 axis=0).astype(jnp.bfloat16)
            kk = jnp.concatenate([kp, kc], axis=0)
            s = lax.dot_general(q2, kk, (((1,), (1,)), ((), ())),
                                preferred_element_type=jnp.float32)
            s = s + bias_ref[jnp.minimum(i, 1)]
            m = jnp.max(s, axis=1, keepdims=True)
            p = jnp.exp2(s - m).astype(jnp.bfloat16)
            vv = jnp.concatenate([jnp.concatenate([vp, vc], axis=0), ones], axis=1)
            pv = jnp.dot(p, vv, preferred_element_type=jnp.float32)
            o_n = pair(pv[:BAND, :LANES], pv[BAND:, :LANES])
            l_n = pair(pv[:BAND, LANES:], pv[BAND:, LANES:])
            m_n = pair(jnp.broadcast_to(m[:BAND], (BAND, LANES)),
                       jnp.broadcast_to(m[BAND:], (BAND, LANES)))
            if multi and not first:
                m_o = m_ref[rows(start), :]
                m_t = jnp.maximum(m_o, m_n)
                a_o = jnp.exp2(m_o - m_t)
                a_n = jnp.exp2(m_n - m_t)
                o_n = acc_ref[rows(start), :] * a_o + o_n * a_n
                l_n = l_ref[rows(start), :] * a_o + l_n * a_n
                m_n = m_t
            if last:
                if with_sink:
                    sink = sink_ref[...]
                    m_t = jnp.maximum(m_n, sink)
                    a_n = jnp.exp2(m_n - m_t)
                    o_n = o_n * a_n
                    l_n = l_n * a_n + jnp.exp2(sink - m_t)
                o_ref[rows(start), :] = (o_n / l_n).astype(o_ref.dtype)
            else:
                acc_ref[rows(start), :] = o_n
                l_ref[rows(start), :] = l_n
                m_ref[rows(start), :] = m_n
            return kc, vc

        def before(r):
            if i_lo == 0:
                zero = jnp.zeros((BAND, LANES), jnp.bfloat16)
                return zero, zero
            start = r + (i_lo - 1) * (BAND * d)
            return (k_ref[rows(start), :].astype(jnp.bfloat16),
                    v_ref[rows(start), :].astype(jnp.bfloat16))

        inner = min(i_hi - i_lo, BLOCKS_IN_FLIGHT)

        def residue(r, _):
            lax.fori_loop(i_lo, i_hi, functools.partial(block, r=r), before(r), unroll=inner)
            return 0

        if d == 1:
            residue(0, 0)
        else:
            assert (i_lo, i_hi) == (0, seq // d // BAND)
            lax.fori_loop(0, d, residue, 0, unroll=BLOCKS_IN_FLIGHT // inner)

    if len(phases) == 1:
        run(*phases[0])
    else:
        for ph, phase in enumerate(phases):
            pl.when(pl.program_id(2) == ph)(functools.partial(run, *phase))


def _band_bias():
    qi = np.arange(BAND)[:, None]
    ki = np.arange(2 * BAND)[None, :]
    dist = BAND + qi - ki
    band = (dist >= 0) & (dist <= BAND)
    full = np.where(band, 0.0, NEG_INF).astype(np.float32)
    head = np.where(band & (ki >= BAND), 0.0, NEG_INF).astype(np.float32)
    both = np.stack([head, full])
    return jnp.asarray(np.concatenate([both, both], axis=1))


def _attention(hq3, q_tile, k_tile, v_tile, shared_kv, phases, sink=None, sample=None):
    b, seq, _ = hq3.shape
    n_pairs = A_W // LANES
    n_ph = len(phases)
    with_sink = sink is not None
    with_sample = sample is not None
    kv_off = (lambda u: 0) if shared_kv else (lambda u: u)
    in_specs = [
        pl.BlockSpec((None, seq, LANES), lambda i, u, j: (i, 0, q_tile + u)),
        pl.BlockSpec((None, seq, LANES), lambda i, u, j: (i, 0, k_tile + kv_off(u))),
        pl.BlockSpec((None, seq, LANES), lambda i, u, j: (i, 0, v_tile + kv_off(u))),
        pl.BlockSpec((2, 2 * BAND, 2 * BAND), lambda i, u, j: (0, 0, 0)),
    ]
    args = [hq3, hq3, hq3, _band_bias()]
    if with_sink:
        in_specs.append(pl.BlockSpec((None, 1, LANES), lambda i, u, j: (u, 0, 0)))
        args.append(sink)
    out_specs = [pl.BlockSpec((None, seq, LANES), lambda i, u, j: (i, 0, u))]
    out_shape = [jax.ShapeDtypeStruct((b, seq, n_pairs * LANES), jnp.bfloat16)]
    if with_sample:
        hq_s, cache_a_t, cache_b_t, sink_col = sample
        n = hq_s.shape[0]
        assert n == b * n_pairs * n_ph, "one sample sequence per grid step"
        buf_a = cache_a_t.shape[-1]
        buf_b = cache_b_t.shape[-1]
        step = lambda i, u, j: (i * n_pairs + u) * n_ph + j
        in_specs += [
            pl.BlockSpec((None, 1, HQ_W), lambda i, u, j: (step(i, u, j), 0, 0)),
            pl.BlockSpec((None, 2, A_W, buf_a), lambda i, u, j: (step(i, u, j), 0, 0, 0)),
            pl.BlockSpec((None, 2, B_KV_W, buf_b), lambda i, u, j: (step(i, u, j), 0, 0, 0)),
            pl.BlockSpec((1, buf_a), lambda i, u, j: (0, 0)),
            pl.BlockSpec((B_HEADS, 1), lambda i, u, j: (0, 0)),
        ]
        args += [hq_s.reshape(n, 1, HQ_W), cache_a_t, cache_b_t, _sample_weights(buf_a), sink_col]
        out_specs += [pl.BlockSpec((None, 1, A_W), lambda i, u, j: (step(i, u, j), 0, 0)),
                      pl.BlockSpec((None, 1, B_W), lambda i, u, j: (step(i, u, j), 0, 0))]
        out_shape += [jax.ShapeDtypeStruct((n, 1, A_W), jnp.float32),
                      jax.ShapeDtypeStruct((n, 1, B_W), jnp.float32)]
    scratch = []
    if len({d for d, _, _ in phases}) > 1:
        scratch = [pltpu.VMEM((seq, LANES), jnp.float32)] * 3
    return pl.pallas_call(
        functools.partial(_attn_kernel, phases=phases, seq=seq, with_sink=with_sink,
                          with_sample=with_sample),
        grid=(b, n_pairs, n_ph),
        in_specs=in_specs,
        out_specs=out_specs,
        out_shape=out_shape,
        scratch_shapes=scratch,
        compiler_params=pltpu.CompilerParams(
            dimension_semantics=("arbitrary", "arbitrary", "arbitrary"),
            vmem_limit_bytes=VMEM_LIMIT),
        name="attn_b" if with_sink else "attn_a",
    )(*args)


def _out_kernel(x_ref, oa_ref, ob_ref, sg_ref, wo_ref, g_ref, b_ref, y_ref, *, alpha):
    sg = sg_ref[...].astype(jnp.float32)
    mix_a = (oa_ref[...].astype(jnp.float32) * sg[:, :A_W]).astype(jnp.bfloat16)
    mix_b = (ob_ref[...].astype(jnp.float32) * sg[:, A_W:]).astype(jnp.bfloat16)
    out = (jnp.dot(mix_a, wo_ref[:A_W, :], preferred_element_type=jnp.float32)
           + jnp.dot(mix_b, wo_ref[A_W:, :], preferred_element_type=jnp.float32))
    z = alpha * x_ref[...] + out
    mu = jnp.mean(z, axis=1, keepdims=True)
    zc = z - mu
    var = jnp.mean(zc * zc, axis=1, keepdims=True)
    y_ref[...] = zc * lax.rsqrt(var + LN_EPS) * g_ref[...] + b_ref[...]


def _output(x2, oa, ob, sg, wo, g, b, alpha, tm):
    n = x2.shape[0]
    row = lambda i: (i, 0)
    fixed = lambda i: (0, 0)
    return pl.pallas_call(
        functools.partial(_out_kernel, alpha=alpha),
        grid=(n // tm,),
        in_specs=[
            pl.BlockSpec((tm, D_MODEL), row),
            pl.BlockSpec((tm, A_W), row),
            pl.BlockSpec((tm, B_W), row),
            pl.BlockSpec((tm, G_W), row),
            pl.BlockSpec((G_W, D_MODEL), fixed),
            pl.BlockSpec((1, D_MODEL), fixed),
            pl.BlockSpec((1, D_MODEL), fixed),
        ],
        out_specs=pl.BlockSpec((tm, D_MODEL), row),
        out_shape=jax.ShapeDtypeStruct((n, D_MODEL), jnp.float32),
        compiler_params=pltpu.CompilerParams(
            dimension_semantics=("arbitrary",), vmem_limit_bytes=VMEM_LIMIT),
        name="out_proj",
    )(x2, oa, ob, sg, wo, g, b)


def _rope_tables(pos):
    inv = ROPE_THETA ** (-jnp.arange(0, ROT_DIM, 2, dtype=jnp.float32) / ROT_DIM)
    ang = pos.astype(jnp.float32)[:, None] * inv[None, :]
    cos = jnp.cos(ang)
    sin = jnp.sin(ang)
    n = pos.shape[0]
    half = ROT_DIM // 2
    pad = HEAD_DIM - ROT_DIM
    one = jnp.ones((n, pad), jnp.float32)
    zero = jnp.zeros((n, pad), jnp.float32)
    zh = jnp.zeros((n, half), jnp.float32)
    cos_h = jnp.concatenate([cos, cos, one], axis=1)
    sinp_h = jnp.concatenate([zh, sin, zero], axis=1)
    sinm_h = jnp.concatenate([-sin, zh, zero], axis=1)
    rep = LANES // HEAD_DIM
    return tuple(jnp.tile(t, (1, rep)) for t in (cos_h, sinp_h, sinm_h))


def _pair_b_heads(a, axis):
    group = B_HEADS // B_KV_HEADS
    shape = a.shape
    split = shape[:axis] + (B_KV_HEADS, group, shape[axis] // B_HEADS) + shape[axis + 1:]
    return jnp.swapaxes(a.reshape(split), axis, axis + 1).reshape(shape)


def _kv_rows(kvt, heads):
    n, _, _, rows = kvt.shape
    return jnp.transpose(kvt.reshape(n, 2, heads, HEAD_DIM, rows), (0, 4, 1, 2, 3))


def _layer(xp, xs, cache_a, cache_b, w_in, sinks, w_o, ln_g, ln_b, alpha, past_len):
    b, seq, _ = xp.shape
    nb = xs.shape[0]
    qa, ka, va, ga, qb, kb, vb, gb = jnp.split(
        w_in, np.cumsum([A_W, A_W, A_W, A_W, B_W, B_KV_W, B_KV_W])[:].tolist(), axis=1)
    q_scale = HEAD_DIM ** -0.5 * LOG2E
    w = jnp.concatenate([qa * q_scale, ka, va, _pair_b_heads(qb, 1) * q_scale, kb, vb,
                         ga, _pair_b_heads(gb, 1)], axis=1).astype(jnp.bfloat16)
    wo = jnp.concatenate([w_o[:A_W], _pair_b_heads(w_o[A_W:], 0)], axis=0).astype(jnp.bfloat16)
    sink_p = _pair_b_heads(sinks.astype(jnp.float32) * LOG2E, 0)
    sink_tiles = jnp.repeat(sink_p, HEAD_DIM).reshape(B_W // LANES, 1, LANES)
    g2 = ln_g.reshape(1, D_MODEL)
    b2 = ln_b.reshape(1, D_MODEL)

    tm = 512
    wa = min(A_WINDOW, seq)
    wb = min(B_WINDOW, seq)
    x2 = xp.reshape(b * seq, D_MODEL)
    hq, sg, kva_t, kvb_t = _project(x2, w, *_rope_tables(jnp.arange(seq)), tm, seq, wa, wb)
    hq3 = hq.reshape(b, seq, HQ_W)

    xs2 = xs.reshape(nb, D_MODEL)
    pos_s = jnp.full((nb,), past_len, jnp.int32)
    hq_s, sg_s, new_a_t, new_b_t = _project(xs2, w, *_rope_tables(pos_s), nb, nb, nb, nb)
    ca_t = jnp.transpose(cache_a, (0, 2, 3, 4, 1)).reshape(nb, 2, A_W, cache_a.shape[1])
    cb_t = jnp.transpose(cache_b, (0, 2, 3, 4, 1)).reshape(nb, 2, B_KV_W, cache_b.shape[1])

    nblk = seq // BAND
    d_hi, d_mid, d_lo = sorted(DILATIONS, reverse=True)
    phases_a = ((d_hi, 0, nblk // d_hi), (d_mid, 0, nblk // d_mid),
                (d_lo, 0, nblk // d_lo // 2), (d_lo, nblk // d_lo // 2, nblk // d_lo))
    oa, oa_s, ob_s = _attention(hq3, QA_T, KA_T, VA_T, False, phases_a,
                                sample=(hq_s, ca_t, cb_t, sink_p.reshape(B_HEADS, 1)))
    ob = _attention(hq3, QB_T, KB_T, VB_T, True, ((1, 0, nblk),), sink=sink_tiles)[0]
    y = _output(x2, oa.reshape(b * seq, A_W), ob.reshape(b * seq, B_W), sg, wo, g2, b2, alpha, 2 * tm)
    kv_a = _kv_rows(kva_t, A_HEADS)
    kv_b = _kv_rows(kvb_t, B_KV_HEADS)

    y_s = _output(xs2, oa_s.reshape(nb, A_W), ob_s.reshape(nb, B_W), sg_s, wo, g2, b2, alpha, nb)
    new_a = _kv_rows(new_a_t, A_HEADS).reshape(nb, 1, 2, A_HEADS, HEAD_DIM)
    new_b = _kv_rows(new_b_t, B_KV_HEADS).reshape(nb, 1, 2, B_KV_HEADS, HEAD_DIM)
    return (y.reshape(b, seq, D_MODEL), y_s.reshape(nb, 1, D_MODEL), kv_a, kv_b, new_a, new_b)


def kernel(x_prompt, x_sample, cache_a_kv, cache_b_kv, w_in, attn_sinks, w_o, ln_g, ln_b):
    depth = w_in.shape[0]
    assert depth == 1 and x_sample.shape[1] == 1, "single layer, one sample token per sequence"
    assert cache_a_kv.shape[2] == BAND * max(DILATIONS), "mixer-A cache covers every strided read"
    assert cache_b_kv.shape[2] == B_WINDOW
    alpha = (2 * depth) ** 0.25
    outs = _layer(x_prompt, x_sample, cache_a_kv[0], cache_b_kv[0], w_in[0], attn_sinks[0],
                  w_o[0], ln_g[0], ln_b[0], alpha, PAST_LEN)
    yp, ys, kv_a, kv_b, new_a, new_b = outs
    return (yp, ys, kv_a[None], kv_b[None], new_a[None], new_b[None])
```

```python
import functools

import jax
import jax.numpy as jnp
import numpy as np
from jax import lax
from jax.experimental import pallas as pl
from jax.experimental.pallas import tpu as pltpu

D_MODEL = 1024
HEAD_DIM = 64
A_HEADS = 8
B_HEADS = 8
B_KV_HEADS = 2
DILATIONS = (1, 4, 16)
BAND = 128
A_WINDOW = 2048
B_WINDOW = 128
PAST_LEN = 16384
ROT_DIM = HEAD_DIM // 4
ROPE_THETA = 500000.0
LN_EPS = 1e-5
A_W = A_HEADS * HEAD_DIM
B_W = B_HEADS * HEAD_DIM
B_KV_W = B_KV_HEADS * HEAD_DIM
LANES = 128
HQ_W = 3 * A_W + B_W + 2 * B_KV_W
G_W = A_W + B_W
IN_WIDTH = HQ_W + G_W
QA_T, KA_T, VA_T = 0, A_W // LANES, 2 * A_W // LANES
QB_T = 3 * A_W // LANES
KB_T = QB_T + B_W // LANES
VB_T = KB_T + 1
VMEM_LIMIT = 56 * 1024 * 1024
COL_CHUNK = 256
NEG_INF = float("-inf")
LOG2E = 1.4426950408889634
BLOCK_COST = {1: 1.0, 4: 1.2, 16: 1.55}
PHASE_SHARE = (0.2, 0.2, 0.22, 0.38)


def _proj_kernel(x_ref, w_ref, cos_ref, sinp_ref, sinm_ref, hq_ref, sg_ref, kva_ref, kvb_ref, *, wb):
    tm = x_ref.shape[0]
    x = x_ref[...].astype(jnp.bfloat16)
    cos = cos_ref[...]
    sinp = sinp_ref[...]
    sinm = sinm_ref[...]
    rope_tiles = set(range(QA_T, VA_T)) | set(range(QB_T, VB_T))
    for c in range(IN_WIDTH // COL_CHUNK):
        acc = jnp.dot(x, w_ref[:, c * COL_CHUNK:(c + 1) * COL_CHUNK],
                      preferred_element_type=jnp.float32)
        for half in range(COL_CHUNK // LANES):
            tile = c * (COL_CHUNK // LANES) + half
            t = acc[:, half * LANES:(half + 1) * LANES]
            if tile < HQ_W // LANES:
                if tile in rope_tiles:
                    t = (t * cos + pltpu.roll(t, ROT_DIM // 2, 1) * sinp
                         + pltpu.roll(t, LANES - ROT_DIM // 2, 1) * sinm)
                hq_ref[:, tile * LANES:(tile + 1) * LANES] = t
                if KA_T <= tile < QB_T:
                    kv, j = divmod(tile - KA_T, A_W // LANES)
                    kva_ref[kv, j * LANES:(j + 1) * LANES, :] = t.T
                elif tile >= KB_T:
                    kvb_ref[tile - KB_T] = t[tm - wb:].T
            else:
                g = tile - HQ_W // LANES
                sg_ref[:, g * LANES:(g + 1) * LANES] = (t * jax.nn.sigmoid(t)).astype(sg_ref.dtype)


def _project(x2, w, cos, sinp, sinm, tm, seq, wa, wb):
    n = x2.shape[0]
    tab_blocks = cos.shape[0] // tm
    tps = seq // tm
    first_a = tps - wa // tm
    assert seq % tm == 0 and wa % tm == 0 and wb <= tm and cos.shape[0] == seq
    row = lambda i: (i, 0)
    tab = lambda i: (i % tab_blocks, 0)
    return pl.pallas_call(
        functools.partial(_proj_kernel, wb=wb),
        grid=(n // tm,),
        in_specs=[
            pl.BlockSpec((tm, D_MODEL), row),
            pl.BlockSpec((D_MODEL, IN_WIDTH), lambda i: (0, 0)),
            pl.BlockSpec((tm, LANES), tab),
            pl.BlockSpec((tm, LANES), tab),
            pl.BlockSpec((tm, LANES), tab),
        ],
        out_specs=[
            pl.BlockSpec((tm, HQ_W), row),
            pl.BlockSpec((tm, G_W), row),
            pl.BlockSpec((None, 2, A_W, tm),
                         lambda i: (i // tps, 0, 0, jnp.maximum(i % tps - first_a, 0))),
            pl.BlockSpec((None, 2, B_KV_W, wb), lambda i: (i // tps, 0, 0, 0)),
        ],
        out_shape=[jax.ShapeDtypeStruct((n, HQ_W), jnp.float32),
                   jax.ShapeDtypeStruct((n, G_W), jnp.bfloat16),
                   jax.ShapeDtypeStruct((n // seq, 2, A_W, wa), jnp.float32),
                   jax.ShapeDtypeStruct((n // seq, 2, B_KV_W, wb), jnp.float32)],
        compiler_params=pltpu.CompilerParams(
            dimension_semantics=("arbitrary",), vmem_limit_bytes=VMEM_LIMIT),
        name="proj",
    )(x2, w, cos, sinp, sinm)


def _lane_lo(shape):
    return lax.broadcasted_iota(jnp.int32, shape, len(shape) - 1) < HEAD_DIM


def _sample_step(hq_ref, ca_ref, cb_ref, w_ref, sink_ref, oa_ref, ob_ref):
    row = hq_ref[...]
    qa = row[:, QA_T * LANES:KA_T * LANES]
    ka_new = row[:, KA_T * LANES:VA_T * LANES]
    va_new = row[:, VA_T * LANES:QB_T * LANES]
    qb = row[:, QB_T * LANES:KB_T * LANES]
    kb_new = row[:, KB_T * LANES:VB_T * LANES]
    vb_new = row[:, VB_T * LANES:HQ_W]
    buf_a = ca_ref.shape[-1]

    hrow = lax.broadcasted_iota(jnp.int32, (A_HEADS, A_W), 0)
    hcol = lax.broadcasted_iota(jnp.int32, (A_HEADS, A_W), 1) // HEAD_DIM
    own = hrow == hcol

    def spread(col):
        return jnp.sum(jnp.where(own, jnp.broadcast_to(col, (A_HEADS, A_W)), 0.0), axis=0, keepdims=True)

    q_bd = jnp.where(own, jnp.broadcast_to(qa, (A_HEADS, A_W)), 0.0)
    s = jnp.dot(q_bd.astype(jnp.bfloat16), ca_ref[0].astype(jnp.bfloat16),
                preferred_element_type=jnp.float32)
    s_new = jnp.sum(q_bd * ka_new, axis=1, keepdims=True)
    w = w_ref[...]
    s = jnp.where(w > 0.0, s, NEG_INF)
    m = jnp.maximum(jnp.max(s, axis=1, keepdims=True), s_new)
    p = jnp.exp2(s - m) * w
    p_new = float(len(DILATIONS)) * jnp.exp2(s_new - m)
    den = jnp.sum(p, axis=1, keepdims=True) + p_new
    folded = []
    for h in range(A_HEADS):
        acc = None
        for t in range(buf_a // LANES):
            cols = slice(t * LANES, (t + 1) * LANES)
            part = ca_ref[1, h * HEAD_DIM:(h + 1) * HEAD_DIM, cols] * p[h:h + 1, cols]
            acc = part if acc is None else acc + part
        folded.append(acc)
    o = jnp.sum(jnp.concatenate(folded, axis=0).T, axis=0, keepdims=True)
    oa_ref[...] = (o + spread(p_new) * va_new) / spread(den)

    prow = lax.broadcasted_iota(jnp.int32, (B_HEADS, LANES), 0)
    pcol = lax.broadcasted_iota(jnp.int32, (B_HEADS, LANES), 1) // HEAD_DIM
    qb_rows = jnp.zeros((B_HEADS, LANES), jnp.float32)
    for t in range(B_W // LANES):
        qb_rows = jnp.where(prow // 2 == t,
                            jnp.broadcast_to(qb[:, t * LANES:(t + 1) * LANES], (B_HEADS, LANES)), qb_rows)
    own_b = (prow % 2) == pcol
    qb_bd = jnp.where(own_b, qb_rows, 0.0)
    sb = jnp.dot(qb_bd.astype(jnp.bfloat16), cb_ref[0].astype(jnp.bfloat16),
                 preferred_element_type=jnp.float32)
    sb_new = jnp.sum(qb_bd * kb_new, axis=1, keepdims=True)
    sink = sink_ref[...]
    mb = jnp.maximum(jnp.maximum(jnp.max(sb, axis=1, keepdims=True), sb_new), sink)
    pb = jnp.exp2(sb - mb)
    pb_new = jnp.exp2(sb_new - mb)
    den_b = jnp.sum(pb, axis=1, keepdims=True) + pb_new + jnp.exp2(sink - mb)
    ob = lax.dot_general(pb.astype(jnp.bfloat16), cb_ref[1].astype(jnp.bfloat16),
                         (((1,), (1,)), ((), ())), preferred_element_type=jnp.float32)
    ob = (ob + pb_new * vb_new) / den_b
    lo = _lane_lo((1, LANES))
    for t in range(B_W // LANES):
        ob_ref[:, t * LANES:(t + 1) * LANES] = jnp.where(lo, ob[2 * t:2 * t + 1], ob[2 * t + 1:2 * t + 2])


def _sample_weights(buf):
    dist = buf - np.arange(buf)
    w = np.zeros((buf,), np.float32)
    for d in DILATIONS:
        w += ((dist % d == 0) & (dist <= BAND * d)).astype(np.float32)
    return jnp.asarray(w[None, :])


def _attn_kernel(*refs, phases, seq, with_sink, with_sample):
    refs = list(refs)
    q_ref, k_ref, v_ref, bias_ref = refs[:4]
    del refs[:4]
    sink_ref = refs.pop(0) if with_sink else None
    sample_in = [refs.pop(0) for _ in range(5)] if with_sample else None
    o_ref = refs.pop(0)
    sample_out = [refs.pop(0) for _ in range(2)] if with_sample else None
    dils = [d for phase in phases for d, _, _, _ in phase]
    multi = len(set(dils)) > 1
    if multi:
        acc_ref, m_ref, l_ref = refs
    lo = _lane_lo((BAND, LANES))
    ones = jnp.ones((2 * BAND, LANES), jnp.bfloat16)

    def pair(a, b):
        return jnp.where(lo, a, b)

    if with_sample:
        _sample_step(*sample_in, *sample_out)

    def run(d, r, i_lo, i_hi):
        first = d == dils[0]
        last = d == dils[-1]

        def rows(start):
            return pl.ds(start, BAND) if d == 1 else pl.ds(start, BAND, stride=d)

        def block(i, carry):
            kp, vp = carry
            start = r + i * (BAND * d)
            q = q_ref[rows(start), :]
            kc = k_ref[rows(start), :].astype(jnp.bfloat16)
            vc = v_ref[rows(start), :].astype(jnp.bfloat16)
            q2 = jnp.concatenate([jnp.where(lo, q, 0.0), jnp.where(lo, 0.0, q)],
                                 axis=0).astype(jnp.bfloat16)
            kk = jnp.concatenate([kp, kc], axis=0)
            s = lax.dot_general(q2, kk, (((1,), (1,)), ((), ())),
                                preferred_element_type=jnp.float32)
            s = s + bias_ref[min(i, 1)]
            m = jnp.max(s, axis=1, keepdims=True)
            p = jnp.exp2(s - m).astype(jnp.bfloat16)
            vv = jnp.concatenate([jnp.concatenate([vp, vc], axis=0), ones], axis=1)
            pv = jnp.dot(p, vv, preferred_element_type=jnp.float32)
            o_n = pair(pv[:BAND, :LANES], pv[BAND:, :LANES])
            l_n = pair(pv[:BAND, LANES:], pv[BAND:, LANES:])
            m_n = pair(jnp.broadcast_to(m[:BAND], (BAND, LANES)),
                       jnp.broadcast_to(m[BAND:], (BAND, LANES)))
            if multi and not first:
                m_o = m_ref[rows(start), :]
                m_t = jnp.maximum(m_o, m_n)
                a_o = jnp.exp2(m_o - m_t)
                a_n = jnp.exp2(m_n - m_t)
                o_n = acc_ref[rows(start), :] * a_o + o_n * a_n
                l_n = l_ref[rows(start), :] * a_o + l_n * a_n
                m_n = m_t
            if last:
                if with_sink:
                    sink = sink_ref[...]
                    m_t = jnp.maximum(m_n, sink)
                    a_n = jnp.exp2(m_n - m_t)
                    o_n = o_n * a_n
                    l_n = l_n * a_n + jnp.exp2(sink - m_t)
                o_ref[rows(start), :] = (o_n / l_n).astype(o_ref.dtype)
            else:
                acc_ref[rows(start), :] = o_n
                l_ref[rows(start), :] = l_n
                m_ref[rows(start), :] = m_n
            return kc, vc

        if i_lo == 0:
            carry = (jnp.zeros((BAND, LANES), jnp.bfloat16),) * 2
        else:
            start = r + (i_lo - 1) * (BAND * d)
            carry = (k_ref[rows(start), :].astype(jnp.bfloat16),
                     v_ref[rows(start), :].astype(jnp.bfloat16))
        for i in range(i_lo, i_hi):
            carry = block(i, carry)

    def run_phase(phase):
        for seg in phase:
            run(*seg)

    if len(phases) == 1:
        run_phase(phases[0])
    else:
        for ph, phase in enumerate(phases):
            pl.when(pl.program_id(2) == ph)(functools.partial(run_phase, phase))


def _band_bias():
    qi = np.arange(BAND)[:, None]
    ki = np.arange(2 * BAND)[None, :]
    dist = BAND + qi - ki
    band = (dist >= 0) & (dist <= BAND)
    full = np.where(band, 0.0, NEG_INF).astype(np.float32)
    head = np.where(band & (ki >= BAND), 0.0, NEG_INF).astype(np.float32)
    both = np.stack([head, full])
    return jnp.asarray(np.concatenate([both, both], axis=1))


def _phase_plan(seq, dilations, shares):
    runs = [(d, r, 0, seq // d // BAND) for d in sorted(dilations, reverse=True) for r in range(d)]
    cost = lambda run: BLOCK_COST[run[0]] * (run[3] - run[2])
    bounds = np.cumsum(shares) * sum(cost(run) for run in runs)
    phases, cur, spent = [], [], 0.0
    for run in runs:
        cur.append(run)
        spent += cost(run)
        if len(phases) < len(shares) - 1 and spent >= bounds[len(phases)]:
            phases.append(tuple(cur))
            cur = []
    phases.append(tuple(cur))
    assert len(phases) == len(shares) and all(phases)
    return tuple(phases)


def _attention(hq3, q_tile, k_tile, v_tile, shared_kv, phases, sink=None, sample=None):
    b, seq, _ = hq3.shape
    n_pairs = A_W // LANES
    n_ph = len(phases)
    with_sink = sink is not None
    with_sample = sample is not None
    kv_off = (lambda u: 0) if shared_kv else (lambda u: u)
    in_specs = [
        pl.BlockSpec((None, seq, LANES), lambda i, u, j: (i, 0, q_tile + u)),
        pl.BlockSpec((None, seq, LANES), lambda i, u, j: (i, 0, k_tile + kv_off(u))),
        pl.BlockSpec((None, seq, LANES), lambda i, u, j: (i, 0, v_tile + kv_off(u))),
        pl.BlockSpec((2, 2 * BAND, 2 * BAND), lambda i, u, j: (0, 0, 0)),
    ]
    args = [hq3, hq3, hq3, _band_bias()]
    if with_sink:
        in_specs.append(pl.BlockSpec((None, 1, LANES), lambda i, u, j: (u, 0, 0)))
        args.append(sink)
    out_specs = [pl.BlockSpec((None, seq, LANES), lambda i, u, j: (i, 0, u))]
    out_shape = [jax.ShapeDtypeStruct((b, seq, n_pairs * LANES), jnp.bfloat16)]
    if with_sample:
        hq_s, cache_a_t, cache_b_t, sink_col = sample
        n = hq_s.shape[0]
        assert n == b * n_pairs * n_ph, "one sample sequence per grid step"
        buf_a = cache_a_t.shape[-1]
        buf_b = cache_b_t.shape[-1]
        step = lambda i, u, j: (i * n_pairs + u) * n_ph + j
        in_specs += [
            pl.BlockSpec((None, 1, HQ_W), lambda i, u, j: (step(i, u, j), 0, 0)),
            pl.BlockSpec((None, 2, A_W, buf_a), lambda i, u, j: (step(i, u, j), 0, 0, 0)),
            pl.BlockSpec((None, 2, B_KV_W, buf_b), lambda i, u, j: (step(i, u, j), 0, 0, 0)),
            pl.BlockSpec((1, buf_a), lambda i, u, j: (0, 0)),
            pl.BlockSpec((B_HEADS, 1), lambda i, u, j: (0, 0)),
        ]
        args += [hq_s.reshape(n, 1, HQ_W), cache_a_t, cache_b_t, _sample_weights(buf_a), sink_col]
        out_specs += [pl.BlockSpec((None, 1, A_W), lambda i, u, j: (step(i, u, j), 0, 0)),
                      pl.BlockSpec((None, 1, B_W), lambda i, u, j: (step(i, u, j), 0, 0))]
        out_shape += [jax.ShapeDtypeStruct((n, 1, A_W), jnp.float32),
                      jax.ShapeDtypeStruct((n, 1, B_W), jnp.float32)]
    scratch = []
    if len({d for phase in phases for d, _, _, _ in phase}) > 1:
        scratch = [pltpu.VMEM((seq, LANES), jnp.float32)] * 3
    return pl.pallas_call(
        functools.partial(_attn_kernel, phases=phases, seq=seq, with_sink=with_sink,
                          with_sample=with_sample),
        grid=(b, n_pairs, n_ph),
        in_specs=in_specs,
        out_specs=out_specs,
        out_shape=out_shape,
        scratch_shapes=scratch,
        compiler_params=pltpu.CompilerParams(
            dimension_semantics=("arbitrary", "arbitrary", "arbitrary"),
            vmem_limit_bytes=VMEM_LIMIT),
        name="attn_b" if with_sink else "attn_a",
    )(*args)


def _out_kernel(x_ref, oa_ref, ob_ref, sg_ref, wo_ref, g_ref, b_ref, y_ref, *, alpha):
    sg = sg_ref[...].astype(jnp.float32)
    mix_a = (oa_ref[...].astype(jnp.float32) * sg[:, :A_W]).astype(jnp.bfloat16)
    mix_b = (ob_ref[...].astype(jnp.float32) * sg[:, A_W:]).astype(jnp.bfloat16)
    out = (jnp.dot(mix_a, wo_ref[:A_W, :], preferred_element_type=jnp.float32)
           + jnp.dot(mix_b, wo_ref[A_W:, :], preferred_element_type=jnp.float32))
    z = alpha * x_ref[...] + out
    mu = jnp.mean(z, axis=1, keepdims=True)
    zc = z - mu
    var = jnp.mean(zc * zc, axis=1, keepdims=True)
    y_ref[...] = zc * lax.rsqrt(var + LN_EPS) * g_ref[...] + b_ref[...]


def _output(x2, oa, ob, sg, wo, g, b, alpha, tm):
    n = x2.shape[0]
    row = lambda i: (i, 0)
    fixed = lambda i: (0, 0)
    return pl.pallas_call(
        functools.partial(_out_kernel, alpha=alpha),
        grid=(n // tm,),
        in_specs=[
            pl.BlockSpec((tm, D_MODEL), row),
            pl.BlockSpec((tm, A_W), row),
            pl.BlockSpec((tm, B_W), row),
            pl.BlockSpec((tm, G_W), row),
            pl.BlockSpec((G_W, D_MODEL), fixed),
            pl.BlockSpec((1, D_MODEL), fixed),
            pl.BlockSpec((1, D_MODEL), fixed),
        ],
        out_specs=pl.BlockSpec((tm, D_MODEL), row),
        out_shape=jax.ShapeDtypeStruct((n, D_MODEL), jnp.float32),
        compiler_params=pltpu.CompilerParams(
            dimension_semantics=("arbitrary",), vmem_limit_bytes=VMEM_LIMIT),
        name="out_proj",
    )(x2, oa, ob, sg, wo, g, b)


def _rope_tables(pos):
    inv = ROPE_THETA ** (-jnp.arange(0, ROT_DIM, 2, dtype=jnp.float32) / ROT_DIM)
    ang = pos.astype(jnp.float32)[:, None] * inv[None, :]
    cos = jnp.cos(ang)
    sin = jnp.sin(ang)
    n = pos.shape[0]
    half = ROT_DIM // 2
    pad = HEAD_DIM - ROT_DIM
    one = jnp.ones((n, pad), jnp.float32)
    zero = jnp.zeros((n, pad), jnp.float32)
    zh = jnp.zeros((n, half), jnp.float32)
    cos_h = jnp.concatenate([cos, cos, one], axis=1)
    sinp_h = jnp.concatenate([zh, sin, zero], axis=1)
    sinm_h = jnp.concatenate([-sin, zh, zero], axis=1)
    rep = LANES // HEAD_DIM
    return tuple(jnp.tile(t, (1, rep)) for t in (cos_h, sinp_h, sinm_h))


def _pair_b_heads(a, axis):
    group = B_HEADS // B_KV_HEADS
    shape = a.shape
    split = shape[:axis] + (B_KV_HEADS, group, shape[axis] // B_HEADS) + shape[axis + 1:]
    return jnp.swapaxes(a.reshape(split), axis, axis + 1).reshape(shape)


def _kv_rows(kvt, heads):
    n, _, _, rows = kvt.shape
    return jnp.transpose(kvt.reshape(n, 2, heads, HEAD_DIM, rows), (0, 4, 1, 2, 3))


def _layer(xp, xs, cache_a, cache_b, w_in, sinks, w_o, ln_g, ln_b, alpha, past_len):
    b, seq, _ = xp.shape
    nb = xs.shape[0]
    qa, ka, va, ga, qb, kb, vb, gb = jnp.split(
        w_in, np.cumsum([A_W, A_W, A_W, A_W, B_W, B_KV_W, B_KV_W])[:].tolist(), axis=1)
    q_scale = HEAD_DIM ** -0.5 * LOG2E
    w = jnp.concatenate([qa * q_scale, ka, va, _pair_b_heads(qb, 1) * q_scale, kb, vb,
                         ga, _pair_b_heads(gb, 1)], axis=1).astype(jnp.bfloat16)
    wo = jnp.concatenate([w_o[:A_W], _pair_b_heads(w_o[A_W:], 0)], axis=0).astype(jnp.bfloat16)
    sink_p = _pair_b_heads(sinks.astype(jnp.float32) * LOG2E, 0)
    sink_tiles = jnp.repeat(sink_p, HEAD_DIM).reshape(B_W // LANES, 1, LANES)
    g2 = ln_g.reshape(1, D_MODEL)
    b2 = ln_b.reshape(1, D_MODEL)

    tm = 512
    wa = min(A_WINDOW, seq)
    wb = min(B_WINDOW, seq)
    x2 = xp.reshape(b * seq, D_MODEL)
    hq, sg, kva_t, kvb_t = _project(x2, w, *_rope_tables(jnp.arange(seq)), tm, seq, wa, wb)
    hq3 = hq.reshape(b, seq, HQ_W)

    xs2 = xs.reshape(nb, D_MODEL)
    pos_s = jnp.full((nb,), past_len, jnp.int32)
    hq_s, sg_s, new_a_t, new_b_t = _project(xs2, w, *_rope_tables(pos_s), nb, nb, nb, nb)
    ca_t = jnp.transpose(cache_a, (0, 2, 3, 4, 1)).reshape(nb, 2, A_W, cache_a.shape[1])
    cb_t = jnp.transpose(cache_b, (0, 2, 3, 4, 1)).reshape(nb, 2, B_KV_W, cache_b.shape[1])

    oa, oa_s, ob_s = _attention(hq3, QA_T, KA_T, VA_T, False, _phase_plan(seq, DILATIONS, PHASE_SHARE),
                                sample=(hq_s, ca_t, cb_t, sink_p.reshape(B_HEADS, 1)))
    ob = _attention(hq3, QB_T, KB_T, VB_T, True, _phase_plan(seq, (1,), (1.0,)), sink=sink_tiles)[0]
    y = _output(x2, oa.reshape(b * seq, A_W), ob.reshape(b * seq, B_W), sg, wo, g2, b2, alpha, 2 * tm)
    kv_a = _kv_rows(kva_t, A_HEADS)
    kv_b = _kv_rows(kvb_t, B_KV_HEADS)

    y_s = _output(xs2, oa_s.reshape(nb, A_W), ob_s.reshape(nb, B_W), sg_s, wo, g2, b2, alpha, nb)
    new_a = _kv_rows(new_a_t, A_HEADS).reshape(nb, 1, 2, A_HEADS, HEAD_DIM)
    new_b = _kv_rows(new_b_t, B_KV_HEADS).reshape(nb, 1, 2, B_KV_HEADS, HEAD_DIM)
    return (y.reshape(b, seq, D_MODEL), y_s.reshape(nb, 1, D_MODEL), kv_a, kv_b, new_a, new_b)


def kernel(x_prompt, x_sample, cache_a_kv, cache_b_kv, w_in, attn_sinks, w_o, ln_g, ln_b):
    depth = w_in.shape[0]
    assert depth == 1 and x_sample.shape[1] == 1, "single layer, one sample token per sequence"
    assert cache_a_kv.shape[2] == BAND * max(DILATIONS), "mixer-A cache covers every strided read"
    assert cache_b_kv.shape[2] == B_WINDOW
    alpha = (2 * depth) ** 0.25
    outs = _layer(x_prompt, x_sample, cache_a_kv[0], cache_b_kv[0], w_in[0], attn_sinks[0],
                  w_o[0], ln_g[0], ln_b[0], alpha, PAST_LEN)
    yp, ys, kv_a, kv_b, new_a, new_b = outs
    return (yp, ys, kv_a[None], kv_b[None], new_a[None], new_b[None])
```

```python
import functools

import jax
import jax.numpy as jnp
import numpy as np
from jax import lax
from jax.experimental import pallas as pl
from jax.experimental.pallas import tpu as pltpu

D_MODEL = 1024
HEAD_DIM = 64
A_HEADS = 8
B_HEADS = 8
B_KV_HEADS = 2
DILATIONS = (1, 4, 16)
BAND = 128
A_WINDOW = 2048
B_WINDOW = 128
PAST_LEN = 16384
ROT_DIM = HEAD_DIM // 4
ROPE_THETA = 500000.0
LN_EPS = 1e-5
A_W = A_HEADS * HEAD_DIM
B_W = B_HEADS * HEAD_DIM
B_KV_W = B_KV_HEADS * HEAD_DIM
LANES = 128
HQ_W = 3 * A_W + B_W + 2 * B_KV_W
G_W = A_W + B_W
IN_WIDTH = HQ_W + G_W
QA_T, KA_T, VA_T = 0, A_W // LANES, 2 * A_W // LANES
QB_T = 3 * A_W // LANES
KB_T = QB_T + B_W // LANES
VB_T = KB_T + 1
VMEM_LIMIT = 56 * 1024 * 1024
COL_CHUNK = 256
NEG_INF = float("-inf")
LOG2E = 1.4426950408889634
BLOCK_COST = {1: 1.0, 4: 1.2, 16: 1.55}
PHASE_SHARE = (0.2, 0.2, 0.22, 0.38)


def _proj_kernel(x_ref, w_ref, cos_ref, sinp_ref, sinm_ref, hq_ref, sg_ref, kva_ref, kvb_ref, *, wb):
    tm = x_ref.shape[0]
    x = x_ref[...].astype(jnp.bfloat16)
    cos = cos_ref[...]
    sinp = sinp_ref[...]
    sinm = sinm_ref[...]
    rope_tiles = set(range(QA_T, VA_T)) | set(range(QB_T, VB_T))
    for c in range(IN_WIDTH // COL_CHUNK):
        acc = jnp.dot(x, w_ref[:, c * COL_CHUNK:(c + 1) * COL_CHUNK],
                      preferred_element_type=jnp.float32)
        for half in range(COL_CHUNK // LANES):
            tile = c * (COL_CHUNK // LANES) + half
            t = acc[:, half * LANES:(half + 1) * LANES]
            if tile < HQ_W // LANES:
                if tile in rope_tiles:
                    t = (t * cos + pltpu.roll(t, ROT_DIM // 2, 1) * sinp
                         + pltpu.roll(t, LANES - ROT_DIM // 2, 1) * sinm)
                hq_ref[:, tile * LANES:(tile + 1) * LANES] = t
                if KA_T <= tile < QB_T:
                    kv, j = divmod(tile - KA_T, A_W // LANES)
                    kva_ref[kv, j * LANES:(j + 1) * LANES, :] = t.T
                elif tile >= KB_T:
                    kvb_ref[tile - KB_T] = t[tm - wb:].T
            else:
                g = tile - HQ_W // LANES
                sg_ref[:, g * LANES:(g + 1) * LANES] = (t * jax.nn.sigmoid(t)).astype(sg_ref.dtype)


def _project(x2, w, cos, sinp, sinm, tm, seq, wa, wb):
    n = x2.shape[0]
    tab_blocks = cos.shape[0] // tm
    tps = seq // tm
    first_a = tps - wa // tm
    assert seq % tm == 0 and wa % tm == 0 and wb <= tm and cos.shape[0] == seq
    row = lambda i: (i, 0)
    tab = lambda i: (i % tab_blocks, 0)
    return pl.pallas_call(
        functools.partial(_proj_kernel, wb=wb),
        grid=(n // tm,),
        in_specs=[
            pl.BlockSpec((tm, D_MODEL), row),
            pl.BlockSpec((D_MODEL, IN_WIDTH), lambda i: (0, 0)),
            pl.BlockSpec((tm, LANES), tab),
            pl.BlockSpec((tm, LANES), tab),
            pl.BlockSpec((tm, LANES), tab),
        ],
        out_specs=[
            pl.BlockSpec((tm, HQ_W), row),
            pl.BlockSpec((tm, G_W), row),
            pl.BlockSpec((None, 2, A_W, tm),
                         lambda i: (i // tps, 0, 0, jnp.maximum(i % tps - first_a, 0))),
            pl.BlockSpec((None, 2, B_KV_W, wb), lambda i: (i // tps, 0, 0, 0)),
        ],
        out_shape=[jax.ShapeDtypeStruct((n, HQ_W), jnp.float32),
                   jax.ShapeDtypeStruct((n, G_W), jnp.bfloat16),
                   jax.ShapeDtypeStruct((n // seq, 2, A_W, wa), jnp.float32),
                   jax.ShapeDtypeStruct((n // seq, 2, B_KV_W, wb), jnp.float32)],
        compiler_params=pltpu.CompilerParams(
            dimension_semantics=("arbitrary",), vmem_limit_bytes=VMEM_LIMIT),
        name="proj",
    )(x2, w, cos, sinp, sinm)


def _lane_lo(shape):
    return lax.broadcasted_iota(jnp.int32, shape, len(shape) - 1) < HEAD_DIM


def _sample_step(hq_ref, ca_ref, cb_ref, w_ref, sink_ref, oa_ref, ob_ref):
    row = hq_ref[...]
    qa = row[:, QA_T * LANES:KA_T * LANES]
    ka_new = row[:, KA_T * LANES:VA_T * LANES]
    va_new = row[:, VA_T * LANES:QB_T * LANES]
    qb = row[:, QB_T * LANES:KB_T * LANES]
    kb_new = row[:, KB_T * LANES:VB_T * LANES]
    vb_new = row[:, VB_T * LANES:HQ_W]
    buf_a = ca_ref.shape[-1]

    hrow = lax.broadcasted_iota(jnp.int32, (A_HEADS, A_W), 0)
    hcol = lax.broadcasted_iota(jnp.int32, (A_HEADS, A_W), 1) // HEAD_DIM
    own = hrow == hcol

    def spread(col):
        return jnp.sum(jnp.where(own, jnp.broadcast_to(col, (A_HEADS, A_W)), 0.0), axis=0, keepdims=True)

    q_bd = jnp.where(own, jnp.broadcast_to(qa, (A_HEADS, A_W)), 0.0)
    s_new = jnp.sum(q_bd * ka_new, axis=1, keepdims=True)
    w = w_ref[...]
    s = jnp.dot(q_bd.astype(jnp.bfloat16), ca_ref[0].astype(jnp.bfloat16),
                preferred_element_type=jnp.float32)
    s = jnp.where(w > 0.0, s, NEG_INF)
    m = jnp.maximum(jnp.max(s, axis=1, keepdims=True), s_new)
    p = jnp.exp2(s - m) * w
    p_new = float(len(DILATIONS)) * jnp.exp2(s_new - m)
    den = jnp.sum(p, axis=1, keepdims=True) + p_new
    folded = []
    for h in range(A_HEADS):
        acc = None
        for t in range(buf_a // LANES):
            cols = slice(t * LANES, (t + 1) * LANES)
            part = ca_ref[1, h * HEAD_DIM:(h + 1) * HEAD_DIM, cols] * p[h:h + 1, cols]
            acc = part if acc is None else acc + part
        folded.append(acc)
    o = jnp.sum(jnp.concatenate(folded, axis=0).T, axis=0, keepdims=True)
    oa_ref[...] = (o + spread(p_new) * va_new) / spread(den)

    prow = lax.broadcasted_iota(jnp.int32, (B_HEADS, LANES), 0)
    pcol = lax.broadcasted_iota(jnp.int32, (B_HEADS, LANES), 1) // HEAD_DIM
    qb_rows = jnp.zeros((B_HEADS, LANES), jnp.float32)
    for t in range(B_W // LANES):
        qb_rows = jnp.where(prow // 2 == t,
                            jnp.broadcast_to(qb[:, t * LANES:(t + 1) * LANES], (B_HEADS, LANES)), qb_rows)
    own_b = (prow % 2) == pcol
    qb_bd = jnp.where(own_b, qb_rows, 0.0)
    sb = jnp.dot(qb_bd.astype(jnp.bfloat16), cb_ref[0].astype(jnp.bfloat16),
                 preferred_element_type=jnp.float32)
    sb_new = jnp.sum(qb_bd * kb_new, axis=1, keepdims=True)
    sink = sink_ref[...]
    mb = jnp.maximum(jnp.maximum(jnp.max(sb, axis=1, keepdims=True), sb_new), sink)
    pb = jnp.exp2(sb - mb)
    pb_new = jnp.exp2(sb_new - mb)
    den_b = jnp.sum(pb, axis=1, keepdims=True) + pb_new + jnp.exp2(sink - mb)
    ob = lax.dot_general(pb.astype(jnp.bfloat16), cb_ref[1].astype(jnp.bfloat16),
                         (((1,), (1,)), ((), ())), preferred_element_type=jnp.float32)
    ob = (ob + pb_new * vb_new) / den_b
    lo = _lane_lo((1, LANES))
    for t in range(B_W // LANES):
        ob_ref[:, t * LANES:(t + 1) * LANES] = jnp.where(lo, ob[2 * t:2 * t + 1], ob[2 * t + 1:2 * t + 2])


def _sample_weights(buf):
    dist = buf - np.arange(buf)
    w = np.zeros((buf,), np.float32)
    for d in DILATIONS:
        w += ((dist % d == 0) & (dist <= BAND * d)).astype(np.float32)
    return jnp.asarray(w[None, :])


def _attn_kernel(*refs, phases, seq, with_sink, with_sample):
    refs = list(refs)
    q_ref, k_ref, v_ref, bias_ref = refs[:4]
    del refs[:4]
    sink_ref = refs.pop(0) if with_sink else None
    sample_in = [refs.pop(0) for _ in range(5)] if with_sample else None
    o_ref = refs.pop(0)
    sample_out = [refs.pop(0) for _ in range(2)] if with_sample else None
    dils = [d for phase in phases for d, _, _, _ in phase]
    multi = len(set(dils)) > 1
    if multi:
        acc_ref, m_ref, l_ref = refs
    lo = _lane_lo((BAND, LANES))

    def pair(a, b):
        return jnp.where(lo, a, b)

    head_lanes = (jnp.where(lo, 1.0, 0.0).astype(jnp.bfloat16),
                  jnp.where(lo, 0.0, 1.0).astype(jnp.bfloat16))

    def by_head(a):
        return a * head_lanes[0], a * head_lanes[1]

    ones = tuple(jnp.concatenate([t, t], axis=0) for t in head_lanes)

    def run(d, r, i_lo, i_hi):
        first = d == dils[0]
        last = d == dils[-1]

        def rows(start):
            return pl.ds(start, BAND) if d == 1 else pl.ds(start, BAND, stride=d)

        def block(i, carry):
            kp, vp = carry
            start = r + i * (BAND * d)
            q = q_ref[rows(start), :]
            kc = k_ref[rows(start), :].astype(jnp.bfloat16)
            vc = by_head(v_ref[rows(start), :].astype(jnp.bfloat16))
            q2 = jnp.concatenate([jnp.where(lo, q, 0.0), jnp.where(lo, 0.0, q)],
                                 axis=0).astype(jnp.bfloat16)
            kk = jnp.concatenate([kp, kc], axis=0)
            s = lax.dot_general(q2, kk, (((1,), (1,)), ((), ())),
                                preferred_element_type=jnp.float32)
            s = s + bias_ref[min(i, 1)]
            m = jnp.max(s, axis=1, keepdims=True)
            p = jnp.exp2(s - m).astype(jnp.bfloat16)
            vv = jnp.concatenate(
                [jnp.concatenate([jnp.concatenate([vp[h], vc[h]], axis=0), ones[h]], axis=1)
                 for h in range(2)], axis=0)
            pv = jnp.dot(jnp.concatenate([p[:BAND], p[BAND:]], axis=1), vv,
                         preferred_element_type=jnp.float32)
            o_n = pv[:, :LANES]
            l_n = pv[:, LANES:]
            m_n = pair(jnp.broadcast_to(m[:BAND], (BAND, LANES)),
                       jnp.broadcast_to(m[BAND:], (BAND, LANES)))
            if multi and not first:
                m_o = m_ref[rows(start), :]
                m_t = jnp.maximum(m_o, m_n)
                a_o = jnp.exp2(m_o - m_t)
                a_n = jnp.exp2(m_n - m_t)
                o_n = acc_ref[rows(start), :] * a_o + o_n * a_n
                l_n = l_ref[rows(start), :] * a_o + l_n * a_n
                m_n = m_t
            if last:
                if with_sink:
                    sink = sink_ref[...]
                    m_t = jnp.maximum(m_n, sink)
                    a_n = jnp.exp2(m_n - m_t)
                    o_n = o_n * a_n
                    l_n = l_n * a_n + jnp.exp2(sink - m_t)
                o_ref[rows(start), :] = (o_n / l_n).astype(o_ref.dtype)
            else:
                acc_ref[rows(start), :] = o_n
                l_ref[rows(start), :] = l_n
                m_ref[rows(start), :] = m_n
            return kc, vc

        if i_lo == 0:
            zero = jnp.zeros((BAND, LANES), jnp.bfloat16)
            carry = (zero, (zero, zero))
        else:
            start = r + (i_lo - 1) * (BAND * d)
            carry = (k_ref[rows(start), :].astype(jnp.bfloat16),
                     by_head(v_ref[rows(start), :].astype(jnp.bfloat16)))
        for i in range(i_lo, i_hi):
            carry = block(i, carry)

    def run_phase(phase):
        if with_sample:
            _sample_step(*sample_in, *sample_out)
        for seg in phase:
            run(*seg)

    if len(phases) == 1:
        run_phase(phases[0])
    else:
        for ph, phase in enumerate(phases):
            pl.when(pl.program_id(2) == ph)(functools.partial(run_phase, phase))


def _band_bias():
    qi = np.arange(BAND)[:, None]
    ki = np.arange(2 * BAND)[None, :]
    dist = BAND + qi - ki
    band = (dist >= 0) & (dist <= BAND)
    full = np.where(band, 0.0, NEG_INF).astype(np.float32)
    head = np.where(band & (ki >= BAND), 0.0, NEG_INF).astype(np.float32)
    both = np.stack([head, full])
    return jnp.asarray(np.concatenate([both, both], axis=1))


def _phase_plan(seq, dilations, shares):
    runs = [(d, r, 0, seq // d // BAND) for d in sorted(dilations, reverse=True) for r in range(d)]
    cost = lambda run: BLOCK_COST[run[0]] * (run[3] - run[2])
    bounds = np.cumsum(shares) * sum(cost(run) for run in runs)
    phases, cur, spent = [], [], 0.0
    for run in runs:
        cur.append(run)
        spent += cost(run)
        if len(phases) < len(shares) - 1 and spent >= bounds[len(phases)]:
            phases.append(tuple(cur))
            cur = []
    phases.append(tuple(cur))
    assert len(phases) == len(shares) and all(phases)
    return tuple(phases)


def _attention(hq3, q_tile, k_tile, v_tile, shared_kv, phases, sink=None, sample=None):
    b, seq, _ = hq3.shape
    n_pairs = A_W // LANES
    n_ph = len(phases)
    with_sink = sink is not None
    with_sample = sample is not None
    kv_off = (lambda u: 0) if shared_kv else (lambda u: u)
    in_specs = [
        pl.BlockSpec((None, seq, LANES), lambda i, u, j: (i, 0, q_tile + u)),
        pl.BlockSpec((None, seq, LANES), lambda i, u, j: (i, 0, k_tile + kv_off(u))),
        pl.BlockSpec((None, seq, LANES), lambda i, u, j: (i, 0, v_tile + kv_off(u))),
        pl.BlockSpec((2, 2 * BAND, 2 * BAND), lambda i, u, j: (0, 0, 0)),
    ]
    args = [hq3, hq3, hq3, _band_bias()]
    if with_sink:
        in_specs.append(pl.BlockSpec((None, 1, LANES), lambda i, u, j: (u, 0, 0)))
        args.append(sink)
    out_specs = [pl.BlockSpec((None, seq, LANES), lambda i, u, j: (i, 0, u))]
    out_shape = [jax.ShapeDtypeStruct((b, seq, n_pairs * LANES), jnp.bfloat16)]
    if with_sample:
        hq_s, cache_a_t, cache_b_t, sink_col = sample
        n = hq_s.shape[0]
        assert n == b * n_pairs * n_ph, "one sample sequence per grid step"
        buf_a = cache_a_t.shape[-1]
        buf_b = cache_b_t.shape[-1]
        step = lambda i, u, j: (i * n_pairs + u) * n_ph + j
        in_specs += [
            pl.BlockSpec((None, 1, HQ_W), lambda i, u, j: (step(i, u, j), 0, 0)),
            pl.BlockSpec((None, 2, A_W, buf_a), lambda i, u, j: (step(i, u, j), 0, 0, 0)),
            pl.BlockSpec((None, 2, B_KV_W, buf_b), lambda i, u, j: (step(i, u, j), 0, 0, 0)),
            pl.BlockSpec((1, buf_a), lambda i, u, j: (0, 0)),
            pl.BlockSpec((B_HEADS, 1), lambda i, u, j: (0, 0)),
        ]
        args += [hq_s.reshape(n, 1, HQ_W), cache_a_t, cache_b_t, _sample_weights(buf_a), sink_col]
        out_specs += [pl.BlockSpec((None, 1, A_W), lambda i, u, j: (step(i, u, j), 0, 0)),
                      pl.BlockSpec((None, 1, B_W), lambda i, u, j: (step(i, u, j), 0, 0))]
        out_shape += [jax.ShapeDtypeStruct((n, 1, A_W), jnp.float32),
                      jax.ShapeDtypeStruct((n, 1, B_W), jnp.float32)]
    scratch = []
    if len({d for phase in phases for d, _, _, _ in phase}) > 1:
        scratch = [pltpu.VMEM((seq, LANES), jnp.float32)] * 3
    return pl.pallas_call(
        functools.partial(_attn_kernel, phases=phases, seq=seq, with_sink=with_sink,
                          with_sample=with_sample),
        grid=(b, n_pairs, n_ph),
        in_specs=in_specs,
        out_specs=out_specs,
        out_shape=out_shape,
        scratch_shapes=scratch,
        compiler_params=pltpu.CompilerParams(
            dimension_semantics=("arbitrary", "arbitrary", "arbitrary"),
            vmem_limit_bytes=VMEM_LIMIT),
        name="attn_b" if with_sink else "attn_a",
    )(*args)


def _out_kernel(x_ref, oa_ref, ob_ref, sg_ref, wo_ref, g_ref, b_ref, y_ref, *, alpha):
    sg = sg_ref[...].astype(jnp.float32)
    mix_a = (oa_ref[...].astype(jnp.float32) * sg[:, :A_W]).astype(jnp.bfloat16)
    mix_b = (ob_ref[...].astype(jnp.float32) * sg[:, A_W:]).astype(jnp.bfloat16)
    out = (jnp.dot(mix_a, wo_ref[:A_W, :], preferred_element_type=jnp.float32)
           + jnp.dot(mix_b, wo_ref[A_W:, :], preferred_element_type=jnp.float32))
    z = alpha * x_ref[...] + out
    mu = jnp.mean(z, axis=1, keepdims=True)
    zc = z - mu
    var = jnp.mean(zc * zc, axis=1, keepdims=True)
    y_ref[...] = zc * lax.rsqrt(var + LN_EPS) * g_ref[...] + b_ref[...]


def _output(x2, oa, ob, sg, wo, g, b, alpha, tm):
    n = x2.shape[0]
    row = lambda i: (i, 0)
    fixed = lambda i: (0, 0)
    return pl.pallas_call(
        functools.partial(_out_kernel, alpha=alpha),
        grid=(n // tm,),
        in_specs=[
            pl.BlockSpec((tm, D_MODEL), row),
            pl.BlockSpec((tm, A_W), row),
            pl.BlockSpec((tm, B_W), row),
            pl.BlockSpec((tm, G_W), row),
            pl.BlockSpec((G_W, D_MODEL), fixed),
            pl.BlockSpec((1, D_MODEL), fixed),
            pl.BlockSpec((1, D_MODEL), fixed),
        ],
        out_specs=pl.BlockSpec((tm, D_MODEL), row),
        out_shape=jax.ShapeDtypeStruct((n, D_MODEL), jnp.float32),
        compiler_params=pltpu.CompilerParams(
            dimension_semantics=("arbitrary",), vmem_limit_bytes=VMEM_LIMIT),
        name="out_proj",
    )(x2, oa, ob, sg, wo, g, b)


def _rope_tables(pos):
    inv = ROPE_THETA ** (-jnp.arange(0, ROT_DIM, 2, dtype=jnp.float32) / ROT_DIM)
    ang = pos.astype(jnp.float32)[:, None] * inv[None, :]
    cos = jnp.cos(ang)
    sin = jnp.sin(ang)
    n = pos.shape[0]
    half = ROT_DIM // 2
    pad = HEAD_DIM - ROT_DIM
    one = jnp.ones((n, pad), jnp.float32)
    zero = jnp.zeros((n, pad), jnp.float32)
    zh = jnp.zeros((n, half), jnp.float32)
    cos_h = jnp.concatenate([cos, cos, one], axis=1)
    sinp_h = jnp.concatenate([zh, sin, zero], axis=1)
    sinm_h = jnp.concatenate([-sin, zh, zero], axis=1)
    rep = LANES // HEAD_DIM
    return tuple(jnp.tile(t, (1, rep)) for t in (cos_h, sinp_h, sinm_h))


def _pair_b_heads(a, axis):
    group = B_HEADS // B_KV_HEADS
    shape = a.shape
    split = shape[:axis] + (B_KV_HEADS, group, shape[axis] // B_HEADS) + shape[axis + 1:]
    return jnp.swapaxes(a.reshape(split), axis, axis + 1).reshape(shape)


def _kv_rows(kvt, heads):
    n, _, _, rows = kvt.shape
    return jnp.transpose(kvt.reshape(n, 2, heads, HEAD_DIM, rows), (0, 4, 1, 2, 3))


def _layer(xp, xs, cache_a, cache_b, w_in, sinks, w_o, ln_g, ln_b, alpha, past_len):
    b, seq, _ = xp.shape
    nb = xs.shape[0]
    qa, ka, va, ga, qb, kb, vb, gb = jnp.split(
        w_in, np.cumsum([A_W, A_W, A_W, A_W, B_W, B_KV_W, B_KV_W])[:].tolist(), axis=1)
    q_scale = HEAD_DIM ** -0.5 * LOG2E
    w = jnp.concatenate([qa * q_scale, ka, va, _pair_b_heads(qb, 1) * q_scale, kb, vb,
                         ga, _pair_b_heads(gb, 1)], axis=1).astype(jnp.bfloat16)
    wo = jnp.concatenate([w_o[:A_W], _pair_b_heads(w_o[A_W:], 0)], axis=0).astype(jnp.bfloat16)
    sink_p = _pair_b_heads(sinks.astype(jnp.float32) * LOG2E, 0)
    sink_tiles = jnp.repeat(sink_p, HEAD_DIM).reshape(B_W // LANES, 1, LANES)
    g2 = ln_g.reshape(1, D_MODEL)
    b2 = ln_b.reshape(1, D_MODEL)

    tm = 512
    wa = min(A_WINDOW, seq)
    wb = min(B_WINDOW, seq)
    x2 = xp.reshape(b * seq, D_MODEL)
    hq, sg, kva_t, kvb_t = _project(x2, w, *_rope_tables(jnp.arange(seq)), tm, seq, wa, wb)
    hq3 = hq.reshape(b, seq, HQ_W)

    xs2 = xs.reshape(nb, D_MODEL)
    pos_s = jnp.full((nb,), past_len, jnp.int32)
    hq_s, sg_s, new_a_t, new_b_t = _project(xs2, w, *_rope_tables(pos_s), nb, nb, nb, nb)
    ca_t = jnp.transpose(cache_a, (0, 2, 3, 4, 1)).reshape(nb, 2, A_W, cache_a.shape[1])
    cb_t = jnp.transpose(cache_b, (0, 2, 3, 4, 1)).reshape(nb, 2, B_KV_W, cache_b.shape[1])

    oa, oa_s, ob_s = _attention(hq3, QA_T, KA_T, VA_T, False, _phase_plan(seq, DILATIONS, PHASE_SHARE),
                                sample=(hq_s, ca_t, cb_t, sink_p.reshape(B_HEADS, 1)))
    ob = _attention(hq3, QB_T, KB_T, VB_T, True, _phase_plan(seq, (1,), (1.0,)), sink=sink_tiles)[0]
    y = _output(x2, oa.reshape(b * seq, A_W), ob.reshape(b * seq, B_W), sg, wo, g2, b2, alpha, 2 * tm)
    kv_a = _kv_rows(kva_t, A_HEADS)
    kv_b = _kv_rows(kvb_t, B_KV_HEADS)

    y_s = _output(xs2, oa_s.reshape(nb, A_W), ob_s.reshape(nb, B_W), sg_s, wo, g2, b2, alpha, nb)
    new_a = _kv_rows(new_a_t, A_HEADS).reshape(nb, 1, 2, A_HEADS, HEAD_DIM)
    new_b = _kv_rows(new_b_t, B_KV_HEADS).reshape(nb, 1, 2, B_KV_HEADS, HEAD_DIM)
    return (y.reshape(b, seq, D_MODEL), y_s.reshape(nb, 1, D_MODEL), kv_a, kv_b, new_a, new_b)


def kernel(x_prompt, x_sample, cache_a_kv, cache_b_kv, w_in, attn_sinks, w_o, ln_g, ln_b):
    depth = w_in.shape[0]
    assert depth == 1 and x_sample.shape[1] == 1, "single layer, one sample token per sequence"
    assert cache_a_kv.shape[2] == BAND * max(DILATIONS), "mixer-A cache covers every strided read"
    assert cache_b_kv.shape[2] == B_WINDOW
    alpha = (2 * depth) ** 0.25
    outs = _layer(x_prompt, x_sample, cache_a_kv[0], cache_b_kv[0], w_in[0], attn_sinks[0],
                  w_o[0], ln_g[0], ln_b[0], alpha, PAST_LEN)
    yp, ys, kv_a, kv_b, new_a, new_b = outs
    return (yp, ys, kv_a[None], kv_b[None], new_a[None], new_b[None])
```

```python
import functools

import jax
import jax.numpy as jnp
import numpy as np
from jax import lax
from jax.experimental import pallas as pl
from jax.experimental.pallas import tpu as pltpu

D_MODEL = 1024
HEAD_DIM = 64
A_HEADS = 8
B_HEADS = 8
B_KV_HEADS = 2
DILATIONS = (1, 4, 16)
BAND = 128
A_WINDOW = 2048
B_WINDOW = 128
PAST_LEN = 16384
ROT_DIM = HEAD_DIM // 4
ROPE_THETA = 500000.0
LN_EPS = 1e-5
A_W = A_HEADS * HEAD_DIM
B_W = B_HEADS * HEAD_DIM
B_KV_W = B_KV_HEADS * HEAD_DIM
LANES = 128
HQ_W = 3 * A_W + B_W + 2 * B_KV_W
G_W = A_W + B_W
IN_WIDTH = HQ_W + G_W
QA_T, KA_T, VA_T = 0, A_W // LANES, 2 * A_W // LANES
QB_T = 3 * A_W // LANES
KB_T = QB_T + B_W // LANES
VB_T = KB_T + 1
VMEM_LIMIT = 56 * 1024 * 1024
COL_CHUNK = 256
NEG_INF = float("-inf")
LOG2E = 1.4426950408889634
BLOCK_COST = {1: 1.0, 4: 1.2, 16: 1.55}
PHASE_SHARE = (0.236, 0.236, 0.236, 0.292)


def _proj_kernel(x_ref, w_ref, cos_ref, sinp_ref, sinm_ref, hq_ref, sg_ref, kva_ref, kvb_ref, *, wb):
    tm = x_ref.shape[0]
    x = x_ref[...].astype(jnp.bfloat16)
    cos = cos_ref[...]
    sinp = sinp_ref[...]
    sinm = sinm_ref[...]
    rope_tiles = set(range(QA_T, VA_T)) | set(range(QB_T, VB_T))
    for c in range(IN_WIDTH // COL_CHUNK):
        acc = jnp.dot(x, w_ref[:, c * COL_CHUNK:(c + 1) * COL_CHUNK],
                      preferred_element_type=jnp.float32)
        for half in range(COL_CHUNK // LANES):
            tile = c * (COL_CHUNK // LANES) + half
            t = acc[:, half * LANES:(half + 1) * LANES]
            if tile < HQ_W // LANES:
                if tile in rope_tiles:
                    t = (t * cos + pltpu.roll(t, ROT_DIM // 2, 1) * sinp
                         + pltpu.roll(t, LANES - ROT_DIM // 2, 1) * sinm)
                hq_ref[:, tile * LANES:(tile + 1) * LANES] = t
                if KA_T <= tile < QB_T:
                    kv, j = divmod(tile - KA_T, A_W // LANES)
                    kva_ref[kv, j * LANES:(j + 1) * LANES, :] = t.T
                elif tile >= KB_T:
                    kvb_ref[tile - KB_T] = t[tm - wb:].T
            else:
                g = tile - HQ_W // LANES
                sg_ref[:, g * LANES:(g + 1) * LANES] = (t * jax.nn.sigmoid(t)).astype(sg_ref.dtype)


def _project(x2, w, cos, sinp, sinm, tm, seq, wa, wb):
    n = x2.shape[0]
    tab_blocks = cos.shape[0] // tm
    tps = seq // tm
    first_a = tps - wa // tm
    assert seq % tm == 0 and wa % tm == 0 and wb <= tm and cos.shape[0] == seq
    row = lambda i: (i, 0)
    tab = lambda i: (i % tab_blocks, 0)
    return pl.pallas_call(
        functools.partial(_proj_kernel, wb=wb),
        grid=(n // tm,),
        in_specs=[
            pl.BlockSpec((tm, D_MODEL), row),
            pl.BlockSpec((D_MODEL, IN_WIDTH), lambda i: (0, 0)),
            pl.BlockSpec((tm, LANES), tab),
            pl.BlockSpec((tm, LANES), tab),
            pl.BlockSpec((tm, LANES), tab),
        ],
        out_specs=[
            pl.BlockSpec((tm, HQ_W), row),
            pl.BlockSpec((tm, G_W), row),
            pl.BlockSpec((None, 2, A_W, tm),
                         lambda i: (i // tps, 0, 0, jnp.maximum(i % tps - first_a, 0))),
            pl.BlockSpec((None, 2, B_KV_W, wb), lambda i: (i // tps, 0, 0, 0)),
        ],
        out_shape=[jax.ShapeDtypeStruct((n, HQ_W), jnp.float32),
                   jax.ShapeDtypeStruct((n, G_W), jnp.bfloat16),
                   jax.ShapeDtypeStruct((n // seq, 2, A_W, wa), jnp.float32),
                   jax.ShapeDtypeStruct((n // seq, 2, B_KV_W, wb), jnp.float32)],
        compiler_params=pltpu.CompilerParams(
            dimension_semantics=("arbitrary",), vmem_limit_bytes=VMEM_LIMIT),
        name="proj",
    )(x2, w, cos, sinp, sinm)


def _lane_lo(shape):
    return lax.broadcasted_iota(jnp.int32, shape, len(shape) - 1) < HEAD_DIM


def _sample_step(hq_ref, ca_ref, cb_ref, w_ref, sink_ref, oa_ref, ob_ref):
    row = hq_ref[...]
    qa = row[:, QA_T * LANES:KA_T * LANES]
    ka_new = row[:, KA_T * LANES:VA_T * LANES]
    va_new = row[:, VA_T * LANES:QB_T * LANES]
    qb = row[:, QB_T * LANES:KB_T * LANES]
    kb_new = row[:, KB_T * LANES:VB_T * LANES]
    vb_new = row[:, VB_T * LANES:HQ_W]
    buf_a = ca_ref.shape[-1]

    hrow = lax.broadcasted_iota(jnp.int32, (A_HEADS, A_W), 0)
    hcol = lax.broadcasted_iota(jnp.int32, (A_HEADS, A_W), 1) // HEAD_DIM
    own = hrow == hcol

    def spread(col):
        return jnp.sum(jnp.where(own, jnp.broadcast_to(col, (A_HEADS, A_W)), 0.0), axis=0, keepdims=True)

    q_bd = jnp.where(own, jnp.broadcast_to(qa, (A_HEADS, A_W)), 0.0)
    s_new = jnp.sum(q_bd * ka_new, axis=1, keepdims=True)
    w = w_ref[...]
    s = jnp.dot(q_bd.astype(jnp.bfloat16), ca_ref[0].astype(jnp.bfloat16),
                preferred_element_type=jnp.float32)
    s = jnp.where(w > 0.0, s, NEG_INF)
    m = jnp.maximum(jnp.max(s, axis=1, keepdims=True), s_new)
    p = jnp.exp2(s - m) * w
    p_new = float(len(DILATIONS)) * jnp.exp2(s_new - m)
    den = jnp.sum(p, axis=1, keepdims=True) + p_new
    folded = []
    for h in range(A_HEADS):
        acc = None
        for t in range(buf_a // LANES):
            cols = slice(t * LANES, (t + 1) * LANES)
            part = ca_ref[1, h * HEAD_DIM:(h + 1) * HEAD_DIM, cols] * p[h:h + 1, cols]
            acc = part if acc is None else acc + part
        folded.append(acc)
    o = jnp.sum(jnp.concatenate(folded, axis=0).T, axis=0, keepdims=True)
    oa_ref[...] = (o + spread(p_new) * va_new) / spread(den)

    prow = lax.broadcasted_iota(jnp.int32, (B_HEADS, LANES), 0)
    pcol = lax.broadcasted_iota(jnp.int32, (B_HEADS, LANES), 1) // HEAD_DIM
    qb_rows = jnp.zeros((B_HEADS, LANES), jnp.float32)
    for t in range(B_W // LANES):
        qb_rows = jnp.where(prow // 2 == t,
                            jnp.broadcast_to(qb[:, t * LANES:(t + 1) * LANES], (B_HEADS, LANES)), qb_rows)
    own_b = (prow % 2) == pcol
    qb_bd = jnp.where(own_b, qb_rows, 0.0)
    sb = jnp.dot(qb_bd.astype(jnp.bfloat16), cb_ref[0].astype(jnp.bfloat16),
                 preferred_element_type=jnp.float32)
    sb_new = jnp.sum(qb_bd * kb_new, axis=1, keepdims=True)
    sink = sink_ref[...]
    mb = jnp.maximum(jnp.maximum(jnp.max(sb, axis=1, keepdims=True), sb_new), sink)
    pb = jnp.exp2(sb - mb)
    pb_new = jnp.exp2(sb_new - mb)
    den_b = jnp.sum(pb, axis=1, keepdims=True) + pb_new + jnp.exp2(sink - mb)
    ob = lax.dot_general(pb.astype(jnp.bfloat16), cb_ref[1].astype(jnp.bfloat16),
                         (((1,), (1,)), ((), ())), preferred_element_type=jnp.float32)
    ob = (ob + pb_new * vb_new) / den_b
    lo = _lane_lo((1, LANES))
    for t in range(B_W // LANES):
        ob_ref[:, t * LANES:(t + 1) * LANES] = jnp.where(lo, ob[2 * t:2 * t + 1], ob[2 * t + 1:2 * t + 2])


def _sample_weights(buf):
    dist = buf - np.arange(buf)
    w = np.zeros((buf,), np.float32)
    for d in DILATIONS:
        w += ((dist % d == 0) & (dist <= BAND * d)).astype(np.float32)
    return jnp.asarray(w[None, :])


def _attn_kernel(*refs, phases, seq, with_sink, sample_phases):
    refs = list(refs)
    q_ref, k_ref, v_ref, bias_ref = refs[:4]
    del refs[:4]
    sink_ref = refs.pop(0) if with_sink else None
    sample_in = [refs.pop(0) for _ in range(5)] if sample_phases else None
    o_ref = refs.pop(0)
    sample_out = [refs.pop(0) for _ in range(2)] if sample_phases else None
    dils = [d for phase in phases for d, _, _, _ in phase]
    multi = len(set(dils)) > 1
    if multi:
        acc_ref, m_ref, l_ref = refs
    lo = _lane_lo((BAND, LANES))

    def pair(a, b):
        return jnp.where(lo, a, b)

    head_lanes = (jnp.where(lo, 1.0, 0.0).astype(jnp.bfloat16),
                  jnp.where(lo, 0.0, 1.0).astype(jnp.bfloat16))

    def by_head(a):
        return a * head_lanes[0], a * head_lanes[1]

    ones = tuple(jnp.concatenate([t, t], axis=0) for t in head_lanes)

    def run(d, r, i_lo, i_hi):
        first = d == dils[0]
        last = d == dils[-1]

        def rows(start):
            return pl.ds(start, BAND) if d == 1 else pl.ds(start, BAND, stride=d)

        def block(i, carry):
            kp, vp = carry
            start = r + i * (BAND * d)
            q = q_ref[rows(start), :]
            kc = k_ref[rows(start), :].astype(jnp.bfloat16)
            vc = by_head(v_ref[rows(start), :].astype(jnp.bfloat16))
            q2 = jnp.concatenate([jnp.where(lo, q, 0.0), jnp.where(lo, 0.0, q)],
                                 axis=0).astype(jnp.bfloat16)
            kk = jnp.concatenate([kp, kc], axis=0)
            s = lax.dot_general(q2, kk, (((1,), (1,)), ((), ())),
                                preferred_element_type=jnp.float32)
            s = s + bias_ref[min(i, 1)]
            m = jnp.max(s, axis=1, keepdims=True)
            p = jnp.exp2(s - m).astype(jnp.bfloat16)
            vv = jnp.concatenate(
                [jnp.concatenate([jnp.concatenate([vp[h], vc[h]], axis=0), ones[h]], axis=1)
                 for h in range(2)], axis=0)
            pv = jnp.dot(jnp.concatenate([p[:BAND], p[BAND:]], axis=1), vv,
                         preferred_element_type=jnp.float32)
            o_n = pv[:, :LANES]
            l_n = pv[:, LANES:]
            m_n = pair(jnp.broadcast_to(m[:BAND], (BAND, LANES)),
                       jnp.broadcast_to(m[BAND:], (BAND, LANES)))
            if multi and not first:
                m_o = m_ref[rows(start), :]
                m_t = jnp.maximum(m_o, m_n)
                a_o = jnp.exp2(m_o - m_t)
                a_n = jnp.exp2(m_n - m_t)
                o_n = acc_ref[rows(start), :] * a_o + o_n * a_n
                l_n = l_ref[rows(start), :] * a_o + l_n * a_n
                m_n = m_t
            if last:
                if with_sink:
                    sink = sink_ref[...]
                    m_t = jnp.maximum(m_n, sink)
                    a_n = jnp.exp2(m_n - m_t)
                    o_n = o_n * a_n
                    l_n = l_n * a_n + jnp.exp2(sink - m_t)
                o_ref[rows(start), :] = (o_n / l_n).astype(o_ref.dtype)
            else:
                acc_ref[rows(start), :] = o_n
                l_ref[rows(start), :] = l_n
                m_ref[rows(start), :] = m_n
            return kc, vc

        if i_lo == 0:
            zero = jnp.zeros((BAND, LANES), jnp.bfloat16)
            carry = (zero, (zero, zero))
        else:
            start = r + (i_lo - 1) * (BAND * d)
            carry = (k_ref[rows(start), :].astype(jnp.bfloat16),
                     by_head(v_ref[rows(start), :].astype(jnp.bfloat16)))
        for i in range(i_lo, i_hi):
            carry = block(i, carry)

    def run_phase(ph):
        if ph < sample_phases:
            _sample_step(*sample_in, *sample_out)
        for seg in phases[ph]:
            run(*seg)

    if len(phases) == 1:
        run_phase(0)
    else:
        for ph in range(len(phases)):
            pl.when(pl.program_id(2) == ph)(functools.partial(run_phase, ph))


def _band_bias():
    qi = np.arange(BAND)[:, None]
    ki = np.arange(2 * BAND)[None, :]
    dist = BAND + qi - ki
    band = (dist >= 0) & (dist <= BAND)
    full = np.where(band, 0.0, NEG_INF).astype(np.float32)
    head = np.where(band & (ki >= BAND), 0.0, NEG_INF).astype(np.float32)
    both = np.stack([head, full])
    return jnp.asarray(np.concatenate([both, both], axis=1))


def _phase_plan(seq, dilations, shares):
    runs = [(d, r, 0, seq // d // BAND) for d in sorted(dilations, reverse=True) for r in range(d)]
    cost = lambda run: BLOCK_COST[run[0]] * (run[3] - run[2])
    bounds = np.cumsum(shares) * sum(cost(run) for run in runs)
    phases, cur, spent = [], [], 0.0
    for run in runs:
        cur.append(run)
        spent += cost(run)
        if len(phases) < len(shares) - 1 and spent >= bounds[len(phases)]:
            phases.append(tuple(cur))
            cur = []
    phases.append(tuple(cur))
    assert len(phases) == len(shares) and all(phases)
    return tuple(phases)


def _attention(hq3, q_tile, k_tile, v_tile, shared_kv, phases, sink=None, sample=None):
    b, seq, _ = hq3.shape
    n_pairs = A_W // LANES
    n_ph = len(phases)
    with_sink = sink is not None
    hosts = sample[5] if sample is not None else 0
    kv_off = (lambda u: 0) if shared_kv else (lambda u: u)
    in_specs = [
        pl.BlockSpec((None, seq, LANES), lambda i, u, j: (i, 0, q_tile + u)),
        pl.BlockSpec((None, seq, LANES), lambda i, u, j: (i, 0, k_tile + kv_off(u))),
        pl.BlockSpec((None, seq, LANES), lambda i, u, j: (i, 0, v_tile + kv_off(u))),
        pl.BlockSpec((2, 2 * BAND, 2 * BAND), lambda i, u, j: (0, 0, 0)),
    ]
    args = [hq3, hq3, hq3, _band_bias()]
    if with_sink:
        in_specs.append(pl.BlockSpec((None, 1, LANES), lambda i, u, j: (u, 0, 0)))
        args.append(sink)
    out_specs = [pl.BlockSpec((None, seq, LANES), lambda i, u, j: (i, 0, u))]
    out_shape = [jax.ShapeDtypeStruct((b, seq, n_pairs * LANES), jnp.bfloat16)]
    if hosts:
        hq_s, cache_a_t, cache_b_t, sink_col, first, _ = sample
        n = b * n_pairs * hosts
        assert 0 < hosts <= n_ph and first + n <= hq_s.shape[0]
        buf_a = cache_a_t.shape[-1]
        buf_b = cache_b_t.shape[-1]
        local = lambda i, u, j: (i * n_pairs + u) * hosts + jnp.minimum(j, hosts - 1)
        in_specs += [
            pl.BlockSpec((None, 1, HQ_W), lambda i, u, j: (first + local(i, u, j), 0, 0)),
            pl.BlockSpec((None, 2, A_W, buf_a), lambda i, u, j: (first + local(i, u, j), 0, 0, 0)),
            pl.BlockSpec((None, 2, B_KV_W, buf_b), lambda i, u, j: (first + local(i, u, j), 0, 0, 0)),
            pl.BlockSpec((1, buf_a), lambda i, u, j: (0, 0)),
            pl.BlockSpec((B_HEADS, 1), lambda i, u, j: (0, 0)),
        ]
        args += [hq_s.reshape(hq_s.shape[0], 1, HQ_W), cache_a_t, cache_b_t,
                 _sample_weights(buf_a), sink_col]
        out_specs += [pl.BlockSpec((None, 1, A_W), lambda i, u, j: (local(i, u, j), 0, 0)),
                      pl.BlockSpec((None, 1, B_W), lambda i, u, j: (local(i, u, j), 0, 0))]
        out_shape += [jax.ShapeDtypeStruct((n, 1, A_W), jnp.float32),
                      jax.ShapeDtypeStruct((n, 1, B_W), jnp.float32)]
    scratch = []
    if len({d for phase in phases for d, _, _, _ in phase}) > 1:
        scratch = [pltpu.VMEM((seq, LANES), jnp.float32)] * 3
    return pl.pallas_call(
        functools.partial(_attn_kernel, phases=phases, seq=seq, with_sink=with_sink,
                          sample_phases=hosts),
        grid=(b, n_pairs, n_ph),
        in_specs=in_specs,
        out_specs=out_specs,
        out_shape=out_shape,
        scratch_shapes=scratch,
        compiler_params=pltpu.CompilerParams(
            dimension_semantics=("arbitrary", "arbitrary", "arbitrary"),
            vmem_limit_bytes=VMEM_LIMIT),
        name="attn_b" if with_sink else "attn_a",
    )(*args)


def _out_kernel(x_ref, oa_ref, ob_ref, sg_ref, wo_ref, g_ref, b_ref, y_ref, *, alpha):
    sg = sg_ref[...].astype(jnp.float32)
    mix_a = (oa_ref[...].astype(jnp.float32) * sg[:, :A_W]).astype(jnp.bfloat16)
    mix_b = (ob_ref[...].astype(jnp.float32) * sg[:, A_W:]).astype(jnp.bfloat16)
    out = (jnp.dot(mix_a, wo_ref[:A_W, :], preferred_element_type=jnp.float32)
           + jnp.dot(mix_b, wo_ref[A_W:, :], preferred_element_type=jnp.float32))
    z = alpha * x_ref[...] + out
    mu = jnp.mean(z, axis=1, keepdims=True)
    zc = z - mu
    var = jnp.mean(zc * zc, axis=1, keepdims=True)
    y_ref[...] = zc * lax.rsqrt(var + LN_EPS) * g_ref[...] + b_ref[...]


def _output(x2, oa, ob, sg, wo, g, b, alpha, tm):
    n = x2.shape[0]
    row = lambda i: (i, 0)
    fixed = lambda i: (0, 0)
    return pl.pallas_call(
        functools.partial(_out_kernel, alpha=alpha),
        grid=(n // tm,),
        in_specs=[
            pl.BlockSpec((tm, D_MODEL), row),
            pl.BlockSpec((tm, A_W), row),
            pl.BlockSpec((tm, B_W), row),
            pl.BlockSpec((tm, G_W), row),
            pl.BlockSpec((G_W, D_MODEL), fixed),
            pl.BlockSpec((1, D_MODEL), fixed),
            pl.BlockSpec((1, D_MODEL), fixed),
        ],
        out_specs=pl.BlockSpec((tm, D_MODEL), row),
        out_shape=jax.ShapeDtypeStruct((n, D_MODEL), jnp.float32),
        compiler_params=pltpu.CompilerParams(
            dimension_semantics=("arbitrary",), vmem_limit_bytes=VMEM_LIMIT),
        name="out_proj",
    )(x2, oa, ob, sg, wo, g, b)


def _rope_tables(pos):
    inv = ROPE_THETA ** (-jnp.arange(0, ROT_DIM, 2, dtype=jnp.float32) / ROT_DIM)
    ang = pos.astype(jnp.float32)[:, None] * inv[None, :]
    cos = jnp.cos(ang)
    sin = jnp.sin(ang)
    n = pos.shape[0]
    half = ROT_DIM // 2
    pad = HEAD_DIM - ROT_DIM
    one = jnp.ones((n, pad), jnp.float32)
    zero = jnp.zeros((n, pad), jnp.float32)
    zh = jnp.zeros((n, half), jnp.float32)
    cos_h = jnp.concatenate([cos, cos, one], axis=1)
    sinp_h = jnp.concatenate([zh, sin, zero], axis=1)
    sinm_h = jnp.concatenate([-sin, zh, zero], axis=1)
    rep = LANES // HEAD_DIM
    return tuple(jnp.tile(t, (1, rep)) for t in (cos_h, sinp_h, sinm_h))


def _pair_b_heads(a, axis):
    group = B_HEADS // B_KV_HEADS
    shape = a.shape
    split = shape[:axis] + (B_KV_HEADS, group, shape[axis] // B_HEADS) + shape[axis + 1:]
    return jnp.swapaxes(a.reshape(split), axis, axis + 1).reshape(shape)


def _kv_rows(kvt, heads):
    n, _, _, rows = kvt.shape
    return jnp.transpose(kvt.reshape(n, 2, heads, HEAD_DIM, rows), (0, 4, 1, 2, 3))


def _layer(xp, xs, cache_a, cache_b, w_in, sinks, w_o, ln_g, ln_b, alpha, past_len):
    b, seq, _ = xp.shape
    nb = xs.shape[0]
    qa, ka, va, ga, qb, kb, vb, gb = jnp.split(
        w_in, np.cumsum([A_W, A_W, A_W, A_W, B_W, B_KV_W, B_KV_W])[:].tolist(), axis=1)
    q_scale = HEAD_DIM ** -0.5 * LOG2E
    w = jnp.concatenate([qa * q_scale, ka, va, _pair_b_heads(qb, 1) * q_scale, kb, vb,
                         ga, _pair_b_heads(gb, 1)], axis=1).astype(jnp.bfloat16)
    wo = jnp.concatenate([w_o[:A_W], _pair_b_heads(w_o[A_W:], 0)], axis=0).astype(jnp.bfloat16)
    sink_p = _pair_b_heads(sinks.astype(jnp.float32) * LOG2E, 0)
    sink_tiles = jnp.repeat(sink_p, HEAD_DIM).reshape(B_W // LANES, 1, LANES)
    g2 = ln_g.reshape(1, D_MODEL)
    b2 = ln_b.reshape(1, D_MODEL)

    tm = 512
    wa = min(A_WINDOW, seq)
    wb = min(B_WINDOW, seq)
    x2 = xp.reshape(b * seq, D_MODEL)
    hq, sg, kva_t, kvb_t = _project(x2, w, *_rope_tables(jnp.arange(seq)), tm, seq, wa, wb)
    hq3 = hq.reshape(b, seq, HQ_W)

    xs2 = xs.reshape(nb, D_MODEL)
    pos_s = jnp.full((nb,), past_len, jnp.int32)
    hq_s, sg_s, new_a_t, new_b_t = _project(xs2, w, *_rope_tables(pos_s), nb, nb, nb, nb)
    ca_t = jnp.transpose(cache_a, (0, 2, 3, 4, 1)).reshape(nb, 2, A_W, cache_a.shape[1])
    cb_t = jnp.transpose(cache_b, (0, 2, 3, 4, 1)).reshape(nb, 2, B_KV_W, cache_b.shape[1])

    sink_col = sink_p.reshape(B_HEADS, 1)
    units = b * (A_W // LANES)
    hosts_a = len(PHASE_SHARE) - 1
    assert nb == units * (hosts_a + 1), "every sample sequence has a hosting grid step"
    oa, oa_s1, ob_s1 = _attention(hq3, QA_T, KA_T, VA_T, False, _phase_plan(seq, DILATIONS, PHASE_SHARE),
                                  sample=(hq_s, ca_t, cb_t, sink_col, 0, hosts_a))
    ob, oa_s2, ob_s2 = _attention(hq3, QB_T, KB_T, VB_T, True, _phase_plan(seq, (1,), (1.0,)),
                                  sink=sink_tiles, sample=(hq_s, ca_t, cb_t, sink_col, units * hosts_a, 1))
    oa_s = jnp.concatenate([oa_s1, oa_s2], axis=0)
    ob_s = jnp.concatenate([ob_s1, ob_s2], axis=0)
    y = _output(x2, oa.reshape(b * seq, A_W), ob.reshape(b * seq, B_W), sg, wo, g2, b2, alpha, 2 * tm)
    kv_a = _kv_rows(kva_t, A_HEADS)
    kv_b = _kv_rows(kvb_t, B_KV_HEADS)

    y_s = _output(xs2, oa_s.reshape(nb, A_W), ob_s.reshape(nb, B_W), sg_s, wo, g2, b2, alpha, nb)
    new_a = _kv_rows(new_a_t, A_HEADS).reshape(nb, 1, 2, A_HEADS, HEAD_DIM)
    new_b = _kv_rows(new_b_t, B_KV_HEADS).reshape(nb, 1, 2, B_KV_HEADS, HEAD_DIM)
    return (y.reshape(b, seq, D_MODEL), y_s.reshape(nb, 1, D_MODEL), kv_a, kv_b, new_a, new_b)


def kernel(x_prompt, x_sample, cache_a_kv, cache_b_kv, w_in, attn_sinks, w_o, ln_g, ln_b):
    depth = w_in.shape[0]
    assert depth == 1 and x_sample.shape[1] == 1, "single layer, one sample token per sequence"
    assert cache_a_kv.shape[2] == BAND * max(DILATIONS), "mixer-A cache covers every strided read"
    assert cache_b_kv.shape[2] == B_WINDOW
    alpha = (2 * depth) ** 0.25
    outs = _layer(x_prompt, x_sample, cache_a_kv[0], cache_b_kv[0], w_in[0], attn_sinks[0],
                  w_o[0], ln_g[0], ln_b[0], alpha, PAST_LEN)
    yp, ys, kv_a, kv_b, new_a, new_b = outs
    return (yp, ys, kv_a[None], kv_b[None], new_a[None], new_b[None])
```

```python
import functools

import jax
import jax.numpy as jnp
import numpy as np
from jax import lax
from jax.experimental import pallas as pl
from jax.experimental.pallas import tpu as pltpu

D_MODEL = 1024
HEAD_DIM = 64
A_HEADS = 8
B_HEADS = 8
B_KV_HEADS = 2
DILATIONS = (1, 4, 16)
BAND = 128
A_WINDOW = 2048
B_WINDOW = 128
PAST_LEN = 16384
ROT_DIM = HEAD_DIM // 4
ROPE_THETA = 500000.0
LN_EPS = 1e-5
A_W = A_HEADS * HEAD_DIM
B_W = B_HEADS * HEAD_DIM
B_KV_W = B_KV_HEADS * HEAD_DIM
LANES = 128
HQ_W = 3 * A_W + B_W + 2 * B_KV_W
G_W = A_W + B_W
IN_WIDTH = HQ_W + G_W
QA_T, KA_T, VA_T = 0, A_W // LANES, 2 * A_W // LANES
QB_T = 3 * A_W // LANES
KB_T = QB_T + B_W // LANES
VB_T = KB_T + 1
VMEM_LIMIT = 56 * 1024 * 1024
COL_CHUNK = 256
NEG_INF = float("-inf")
LOG2E = 1.4426950408889634
BLOCK_COST = {1: 1.0, 4: 1.2, 16: 1.55}
PHASE_SHARE = (0.29, 0.236, 0.237, 0.237)
RUN_BLOCKS = 8


def _proj_kernel(x_ref, w_ref, cos_ref, sinp_ref, sinm_ref, hq_ref, sg_ref, kva_ref, kvb_ref, *, wb):
    tm = x_ref.shape[0]
    x = x_ref[...].astype(jnp.bfloat16)
    cos = cos_ref[...]
    sinp = sinp_ref[...]
    sinm = sinm_ref[...]
    rope_tiles = set(range(QA_T, VA_T)) | set(range(QB_T, VB_T))
    for c in range(IN_WIDTH // COL_CHUNK):
        acc = jnp.dot(x, w_ref[:, c * COL_CHUNK:(c + 1) * COL_CHUNK],
                      preferred_element_type=jnp.float32)
        for half in range(COL_CHUNK // LANES):
            tile = c * (COL_CHUNK // LANES) + half
            t = acc[:, half * LANES:(half + 1) * LANES]
            if tile < HQ_W // LANES:
                if tile in rope_tiles:
                    t = (t * cos + pltpu.roll(t, ROT_DIM // 2, 1) * sinp
                         + pltpu.roll(t, LANES - ROT_DIM // 2, 1) * sinm)
                hq_ref[:, tile * LANES:(tile + 1) * LANES] = t
                if KA_T <= tile < QB_T:
                    kv, j = divmod(tile - KA_T, A_W // LANES)
                    kva_ref[kv, j * LANES:(j + 1) * LANES, :] = t.T
                elif tile >= KB_T:
                    kvb_ref[tile - KB_T] = t[tm - wb:].T
            else:
                g = tile - HQ_W // LANES
                sg_ref[:, g * LANES:(g + 1) * LANES] = (t * jax.nn.sigmoid(t)).astype(sg_ref.dtype)


def _project(x2, w, cos, sinp, sinm, tm, seq, wa, wb):
    n = x2.shape[0]
    tab_blocks = cos.shape[0] // tm
    tps = seq // tm
    first_a = tps - wa // tm
    assert seq % tm == 0 and wa % tm == 0 and wb <= tm and cos.shape[0] == seq
    row = lambda i: (i, 0)
    tab = lambda i: (i % tab_blocks, 0)
    return pl.pallas_call(
        functools.partial(_proj_kernel, wb=wb),
        grid=(n // tm,),
        in_specs=[
            pl.BlockSpec((tm, D_MODEL), row),
            pl.BlockSpec((D_MODEL, IN_WIDTH), lambda i: (0, 0)),
            pl.BlockSpec((tm, LANES), tab),
            pl.BlockSpec((tm, LANES), tab),
            pl.BlockSpec((tm, LANES), tab),
        ],
        out_specs=[
            pl.BlockSpec((tm, HQ_W), row),
            pl.BlockSpec((tm, G_W), row),
            pl.BlockSpec((None, 2, A_W, tm),
                         lambda i: (i // tps, 0, 0, jnp.maximum(i % tps - first_a, 0))),
            pl.BlockSpec((None, 2, B_KV_W, wb), lambda i: (i // tps, 0, 0, 0)),
        ],
        out_shape=[jax.ShapeDtypeStruct((n, HQ_W), jnp.float32),
                   jax.ShapeDtypeStruct((n, G_W), jnp.bfloat16),
                   jax.ShapeDtypeStruct((n // seq, 2, A_W, wa), jnp.float32),
                   jax.ShapeDtypeStruct((n // seq, 2, B_KV_W, wb), jnp.float32)],
        compiler_params=pltpu.CompilerParams(
            dimension_semantics=("arbitrary",), vmem_limit_bytes=VMEM_LIMIT),
        name="proj",
    )(x2, w, cos, sinp, sinm)


def _lane_lo(shape):
    return lax.broadcasted_iota(jnp.int32, shape, len(shape) - 1) < HEAD_DIM


def _sample_step(hq_ref, ca_ref, cb_ref, w_ref, sink_ref, oa_ref, ob_ref):
    row = hq_ref[...]
    qa = row[:, QA_T * LANES:KA_T * LANES]
    ka_new = row[:, KA_T * LANES:VA_T * LANES]
    va_new = row[:, VA_T * LANES:QB_T * LANES]
    qb = row[:, QB_T * LANES:KB_T * LANES]
    kb_new = row[:, KB_T * LANES:VB_T * LANES]
    vb_new = row[:, VB_T * LANES:HQ_W]
    buf_a = ca_ref.shape[-1]

    hrow = lax.broadcasted_iota(jnp.int32, (A_HEADS, A_W), 0)
    hcol = lax.broadcasted_iota(jnp.int32, (A_HEADS, A_W), 1) // HEAD_DIM
    own = hrow == hcol

    def spread(col):
        return jnp.sum(jnp.where(own, jnp.broadcast_to(col, (A_HEADS, A_W)), 0.0), axis=0, keepdims=True)

    q_bd = jnp.where(own, jnp.broadcast_to(qa, (A_HEADS, A_W)), 0.0)
    s_new = jnp.sum(q_bd * ka_new, axis=1, keepdims=True)
    w = w_ref[...]
    s = jnp.dot(q_bd.astype(jnp.bfloat16), ca_ref[0].astype(jnp.bfloat16),
                preferred_element_type=jnp.float32)
    s = jnp.where(w > 0.0, s, NEG_INF)
    m = jnp.maximum(jnp.max(s, axis=1, keepdims=True), s_new)
    p = jnp.exp2(s - m) * w
    p_new = float(len(DILATIONS)) * jnp.exp2(s_new - m)
    den = jnp.sum(p, axis=1, keepdims=True) + p_new
    folded = []
    for h in range(A_HEADS):
        acc = None
        for t in range(buf_a // LANES):
            cols = slice(t * LANES, (t + 1) * LANES)
            part = ca_ref[1, h * HEAD_DIM:(h + 1) * HEAD_DIM, cols] * p[h:h + 1, cols]
            acc = part if acc is None else acc + part
        folded.append(acc)
    o = jnp.sum(jnp.concatenate(folded, axis=0).T, axis=0, keepdims=True)
    oa_ref[...] = (o + spread(p_new) * va_new) / spread(den)

    prow = lax.broadcasted_iota(jnp.int32, (B_HEADS, LANES), 0)
    pcol = lax.broadcasted_iota(jnp.int32, (B_HEADS, LANES), 1) // HEAD_DIM
    qb_rows = jnp.zeros((B_HEADS, LANES), jnp.float32)
    for t in range(B_W // LANES):
        qb_rows = jnp.where(prow // 2 == t,
                            jnp.broadcast_to(qb[:, t * LANES:(t + 1) * LANES], (B_HEADS, LANES)), qb_rows)
    own_b = (prow % 2) == pcol
    qb_bd = jnp.where(own_b, qb_rows, 0.0)
    sb = jnp.dot(qb_bd.astype(jnp.bfloat16), cb_ref[0].astype(jnp.bfloat16),
                 preferred_element_type=jnp.float32)
    sb_new = jnp.sum(qb_bd * kb_new, axis=1, keepdims=True)
    sink = sink_ref[...]
    mb = jnp.maximum(jnp.maximum(jnp.max(sb, axis=1, keepdims=True), sb_new), sink)
    pb = jnp.exp2(sb - mb)
    pb_new = jnp.exp2(sb_new - mb)
    den_b = jnp.sum(pb, axis=1, keepdims=True) + pb_new + jnp.exp2(sink - mb)
    ob = lax.dot_general(pb.astype(jnp.bfloat16), cb_ref[1].astype(jnp.bfloat16),
                         (((1,), (1,)), ((), ())), preferred_element_type=jnp.float32)
    ob = (ob + pb_new * vb_new) / den_b
    lo = _lane_lo((1, LANES))
    for t in range(B_W // LANES):
        ob_ref[:, t * LANES:(t + 1) * LANES] = jnp.where(lo, ob[2 * t:2 * t + 1], ob[2 * t + 1:2 * t + 2])


def _sample_weights(buf):
    dist = buf - np.arange(buf)
    w = np.zeros((buf,), np.float32)
    for d in DILATIONS:
        w += ((dist % d == 0) & (dist <= BAND * d)).astype(np.float32)
    return jnp.asarray(w[None, :])


def _attn_kernel(*refs, phases, seq, with_sink, host_phases):
    refs = list(refs)
    q_ref, k_ref, v_ref, bias_ref = refs[:4]
    del refs[:4]
    sink_ref = refs.pop(0) if with_sink else None
    sample_in = [refs.pop(0) for _ in range(5)] if host_phases else None
    o_ref = refs.pop(0)
    sample_out = [refs.pop(0) for _ in range(2)] if host_phases else None
    dils = [d for phase in phases for d, _, _, _ in phase]
    multi = len(set(dils)) > 1
    if multi:
        acc_ref, m_ref, l_ref = refs
    lo = _lane_lo((BAND, LANES))

    def pair(a, b):
        return jnp.where(lo, a, b)

    head_lanes = (jnp.where(lo, 1.0, 0.0).astype(jnp.bfloat16),
                  jnp.where(lo, 0.0, 1.0).astype(jnp.bfloat16))

    def by_head(a):
        return a * head_lanes[0], a * head_lanes[1]

    ones = tuple(jnp.concatenate([t, t], axis=0) for t in head_lanes)

    def run(d, r, i_lo, i_hi):
        first = d == dils[0]
        last = d == dils[-1]

        def rows(start):
            return pl.ds(start, BAND) if d == 1 else pl.ds(start, BAND, stride=d)

        def block(i, carry):
            kp, vp = carry
            start = r + i * (BAND * d)
            q = q_ref[rows(start), :]
            kc = k_ref[rows(start), :].astype(jnp.bfloat16)
            vc = by_head(v_ref[rows(start), :].astype(jnp.bfloat16))
            q2 = jnp.concatenate([jnp.where(lo, q, 0.0), jnp.where(lo, 0.0, q)],
                                 axis=0).astype(jnp.bfloat16)
            kk = jnp.concatenate([kp, kc], axis=0)
            s = lax.dot_general(q2, kk, (((1,), (1,)), ((), ())),
                                preferred_element_type=jnp.float32)
            s = s + bias_ref[min(i, 1)]
            m = jnp.max(s, axis=1, keepdims=True)
            p = jnp.exp2(s - m).astype(jnp.bfloat16)
            vv = jnp.concatenate(
                [jnp.concatenate([jnp.concatenate([vp[h], vc[h]], axis=0), ones[h]], axis=1)
                 for h in range(2)], axis=0)
            pv = jnp.dot(jnp.concatenate([p[:BAND], p[BAND:]], axis=1), vv,
                         preferred_element_type=jnp.float32)
            o_n = pv[:, :LANES]
            l_n = pv[:, LANES:]
            m_n = pair(jnp.broadcast_to(m[:BAND], (BAND, LANES)),
                       jnp.broadcast_to(m[BAND:], (BAND, LANES)))
            if multi and not first:
                m_o = m_ref[rows(start), :]
                m_t = jnp.maximum(m_o, m_n)
                a_o = jnp.exp2(m_o - m_t)
                a_n = jnp.exp2(m_n - m_t)
                o_n = acc_ref[rows(start), :] * a_o + o_n * a_n
                l_n = l_ref[rows(start), :] * a_o + l_n * a_n
                m_n = m_t
            if last:
                if with_sink:
                    sink = sink_ref[...]
                    m_t = jnp.maximum(m_n, sink)
                    a_n = jnp.exp2(m_n - m_t)
                    o_n = o_n * a_n
                    l_n = l_n * a_n + jnp.exp2(sink - m_t)
                o_ref[rows(start), :] = (o_n / l_n).astype(o_ref.dtype)
            else:
                acc_ref[rows(start), :] = o_n
                l_ref[rows(start), :] = l_n
                m_ref[rows(start), :] = m_n
            return kc, vc

        if i_lo == 0:
            zero = jnp.zeros((BAND, LANES), jnp.bfloat16)
            carry = (zero, (zero, zero))
        else:
            start = r + (i_lo - 1) * (BAND * d)
            carry = (k_ref[rows(start), :].astype(jnp.bfloat16),
                     by_head(v_ref[rows(start), :].astype(jnp.bfloat16)))
        for i in range(i_lo, i_hi):
            carry = block(i, carry)

    def run_phase(ph):
        if ph in host_phases:
            _sample_step(*sample_in, *sample_out)
        for seg in phases[ph]:
            run(*seg)

    if len(phases) == 1:
        run_phase(0)
    else:
        for ph in range(len(phases)):
            pl.when(pl.program_id(2) == ph)(functools.partial(run_phase, ph))


def _band_bias():
    qi = np.arange(BAND)[:, None]
    ki = np.arange(2 * BAND)[None, :]
    dist = BAND + qi - ki
    band = (dist >= 0) & (dist <= BAND)
    full = np.where(band, 0.0, NEG_INF).astype(np.float32)
    head = np.where(band & (ki >= BAND), 0.0, NEG_INF).astype(np.float32)
    both = np.stack([head, full])
    return jnp.asarray(np.concatenate([both, both], axis=1))


def _phase_plan(seq, dilations, shares):
    runs = [(d, r, lo, min(lo + RUN_BLOCKS, seq // d // BAND))
            for d in sorted(dilations, reverse=True) for r in range(d)
            for lo in range(0, seq // d // BAND, RUN_BLOCKS)]
    cost = lambda run: BLOCK_COST[run[0]] * (run[3] - run[2])
    bounds = np.cumsum(shares) * sum(cost(run) for run in runs)
    phases, cur, spent = [], [], 0.0
    for run in runs:
        cur.append(run)
        spent += cost(run)
        if len(phases) < len(shares) - 1 and spent >= bounds[len(phases)]:
            phases.append(tuple(cur))
            cur = []
    phases.append(tuple(cur))
    assert len(phases) == len(shares) and all(phases)
    return tuple(phases)


def _attention(hq3, q_tile, k_tile, v_tile, shared_kv, phases, sink=None, sample=None):
    b, seq, _ = hq3.shape
    n_pairs = A_W // LANES
    n_ph = len(phases)
    with_sink = sink is not None
    hosts = sample[5] if sample is not None else 0
    host_phases = tuple(range(n_ph - hosts, n_ph))
    kv_off = (lambda u: 0) if shared_kv else (lambda u: u)
    in_specs = [
        pl.BlockSpec((None, seq, LANES), lambda i, u, j: (i, 0, q_tile + u)),
        pl.BlockSpec((None, seq, LANES), lambda i, u, j: (i, 0, k_tile + kv_off(u))),
        pl.BlockSpec((None, seq, LANES), lambda i, u, j: (i, 0, v_tile + kv_off(u))),
        pl.BlockSpec((2, 2 * BAND, 2 * BAND), lambda i, u, j: (0, 0, 0)),
    ]
    args = [hq3, hq3, hq3, _band_bias()]
    if with_sink:
        in_specs.append(pl.BlockSpec((None, 1, LANES), lambda i, u, j: (u, 0, 0)))
        args.append(sink)
    out_specs = [pl.BlockSpec((None, seq, LANES), lambda i, u, j: (i, 0, u))]
    out_shape = [jax.ShapeDtypeStruct((b, seq, n_pairs * LANES), jnp.bfloat16)]
    if hosts:
        hq_s, cache_a_t, cache_b_t, sink_col, first, _ = sample
        n = b * n_pairs * hosts
        assert 0 < hosts <= n_ph and first + n <= hq_s.shape[0]
        buf_a = cache_a_t.shape[-1]
        buf_b = cache_b_t.shape[-1]

        def local(i, u, j):
            unit = i * n_pairs + u
            return jnp.maximum(unit * hosts + jnp.maximum(j - (n_ph - hosts), -1), 0)

        in_specs += [
            pl.BlockSpec((None, 1, HQ_W), lambda i, u, j: (first + local(i, u, j), 0, 0)),
            pl.BlockSpec((None, 2, A_W, buf_a), lambda i, u, j: (first + local(i, u, j), 0, 0, 0)),
            pl.BlockSpec((None, 2, B_KV_W, buf_b), lambda i, u, j: (first + local(i, u, j), 0, 0, 0)),
            pl.BlockSpec((1, buf_a), lambda i, u, j: (0, 0)),
            pl.BlockSpec((B_HEADS, 1), lambda i, u, j: (0, 0)),
        ]
        args += [hq_s.reshape(hq_s.shape[0], 1, HQ_W), cache_a_t, cache_b_t,
                 _sample_weights(buf_a), sink_col]
        out_specs += [pl.BlockSpec((None, 1, A_W), lambda i, u, j: (local(i, u, j), 0, 0)),
                      pl.BlockSpec((None, 1, B_W), lambda i, u, j: (local(i, u, j), 0, 0))]
        out_shape += [jax.ShapeDtypeStruct((n, 1, A_W), jnp.float32),
                      jax.ShapeDtypeStruct((n, 1, B_W), jnp.float32)]
    scratch = []
    if len({d for phase in phases for d, _, _, _ in phase}) > 1:
        scratch = [pltpu.VMEM((seq, LANES), jnp.float32)] * 3
    return pl.pallas_call(
        functools.partial(_attn_kernel, phases=phases, seq=seq, with_sink=with_sink,
                          host_phases=host_phases),
        grid=(b, n_pairs, n_ph),
        in_specs=in_specs,
        out_specs=out_specs,
        out_shape=out_shape,
        scratch_shapes=scratch,
        compiler_params=pltpu.CompilerParams(
            dimension_semantics=("arbitrary", "arbitrary", "arbitrary"),
            vmem_limit_bytes=VMEM_LIMIT),
        name="attn_b" if with_sink else "attn_a",
    )(*args)


def _out_kernel(x_ref, oa_ref, ob_ref, sg_ref, wo_ref, g_ref, b_ref, y_ref, *, alpha):
    sg = sg_ref[...].astype(jnp.float32)
    mix_a = (oa_ref[...].astype(jnp.float32) * sg[:, :A_W]).astype(jnp.bfloat16)
    mix_b = (ob_ref[...].astype(jnp.float32) * sg[:, A_W:]).astype(jnp.bfloat16)
    out = (jnp.dot(mix_a, wo_ref[:A_W, :], preferred_element_type=jnp.float32)
           + jnp.dot(mix_b, wo_ref[A_W:, :], preferred_element_type=jnp.float32))
    z = alpha * x_ref[...] + out
    mu = jnp.mean(z, axis=1, keepdims=True)
    zc = z - mu
    var = jnp.mean(zc * zc, axis=1, keepdims=True)
    y_ref[...] = zc * lax.rsqrt(var + LN_EPS) * g_ref[...] + b_ref[...]


def _output(x2, oa, ob, sg, wo, g, b, alpha, tm):
    n = x2.shape[0]
    row = lambda i: (i, 0)
    fixed = lambda i: (0, 0)
    return pl.pallas_call(
        functools.partial(_out_kernel, alpha=alpha),
        grid=(n // tm,),
        in_specs=[
            pl.BlockSpec((tm, D_MODEL), row),
            pl.BlockSpec((tm, A_W), row),
            pl.BlockSpec((tm, B_W), row),
            pl.BlockSpec((tm, G_W), row),
            pl.BlockSpec((G_W, D_MODEL), fixed),
            pl.BlockSpec((1, D_MODEL), fixed),
            pl.BlockSpec((1, D_MODEL), fixed),
        ],
        out_specs=pl.BlockSpec((tm, D_MODEL), row),
        out_shape=jax.ShapeDtypeStruct((n, D_MODEL), jnp.float32),
        compiler_params=pltpu.CompilerParams(
            dimension_semantics=("arbitrary",), vmem_limit_bytes=VMEM_LIMIT),
        name="out_proj",
    )(x2, oa, ob, sg, wo, g, b)


def _rope_tables(pos):
    inv = ROPE_THETA ** (-np.arange(0, ROT_DIM, 2, dtype=np.float64) / ROT_DIM)
    ang = pos.astype(np.float64)[:, None] * inv[None, :]
    cos = np.cos(ang)
    sin = np.sin(ang)
    n = pos.shape[0]
    half = ROT_DIM // 2
    pad = HEAD_DIM - ROT_DIM
    one = np.ones((n, pad))
    zero = np.zeros((n, pad))
    zh = np.zeros((n, half))
    cos_h = np.concatenate([cos, cos, one], axis=1)
    sinp_h = np.concatenate([zh, sin, zero], axis=1)
    sinm_h = np.concatenate([-sin, zh, zero], axis=1)
    rep = LANES // HEAD_DIM
    return tuple(jnp.asarray(np.tile(t, (1, rep)), jnp.float32) for t in (cos_h, sinp_h, sinm_h))


def _pair_b_heads(a, axis):
    group = B_HEADS // B_KV_HEADS
    shape = a.shape
    split = shape[:axis] + (B_KV_HEADS, group, shape[axis] // B_HEADS) + shape[axis + 1:]
    return jnp.swapaxes(a.reshape(split), axis, axis + 1).reshape(shape)


def _kv_rows(kvt, heads):
    n, _, _, rows = kvt.shape
    return jnp.transpose(kvt.reshape(n, 2, heads, HEAD_DIM, rows), (0, 4, 1, 2, 3))


def _layer(xp, xs, cache_a, cache_b, w_in, sinks, w_o, ln_g, ln_b, alpha, past_len):
    b, seq, _ = xp.shape
    nb = xs.shape[0]
    qa, ka, va, ga, qb, kb, vb, gb = jnp.split(
        w_in, np.cumsum([A_W, A_W, A_W, A_W, B_W, B_KV_W, B_KV_W])[:].tolist(), axis=1)
    q_scale = HEAD_DIM ** -0.5 * LOG2E
    w = jnp.concatenate([qa * q_scale, ka, va, _pair_b_heads(qb, 1) * q_scale, kb, vb,
                         ga, _pair_b_heads(gb, 1)], axis=1).astype(jnp.bfloat16)
    wo = jnp.concatenate([w_o[:A_W], _pair_b_heads(w_o[A_W:], 0)], axis=0).astype(jnp.bfloat16)
    sink_p = _pair_b_heads(sinks.astype(jnp.float32) * LOG2E, 0)
    sink_tiles = jnp.repeat(sink_p, HEAD_DIM).reshape(B_W // LANES, 1, LANES)
    g2 = ln_g.reshape(1, D_MODEL)
    b2 = ln_b.reshape(1, D_MODEL)

    tm = 512
    wa = min(A_WINDOW, seq)
    wb = min(B_WINDOW, seq)
    x2 = xp.reshape(b * seq, D_MODEL)
    hq, sg, kva_t, kvb_t = _project(x2, w, *_rope_tables(np.arange(seq)), tm, seq, wa, wb)
    hq3 = hq.reshape(b, seq, HQ_W)

    xs2 = xs.reshape(nb, D_MODEL)
    pos_s = np.full((nb,), past_len, np.int64)
    hq_s, sg_s, new_a_t, new_b_t = _project(xs2, w, *_rope_tables(pos_s), nb, nb, nb, nb)
    ca_t = jnp.transpose(cache_a, (0, 2, 3, 4, 1)).reshape(nb, 2, A_W, cache_a.shape[1])
    cb_t = jnp.transpose(cache_b, (0, 2, 3, 4, 1)).reshape(nb, 2, B_KV_W, cache_b.shape[1])

    sink_col = sink_p.reshape(B_HEADS, 1)
    units = b * (A_W // LANES)
    hosts_a = len(PHASE_SHARE) - 1
    assert nb == units * (hosts_a + 1), "every sample sequence has a hosting grid step"
    oa, oa_s1, ob_s1 = _attention(hq3, QA_T, KA_T, VA_T, False, _phase_plan(seq, DILATIONS, PHASE_SHARE),
                                  sample=(hq_s, ca_t, cb_t, sink_col, 0, hosts_a))
    ob, oa_s2, ob_s2 = _attention(hq3, QB_T, KB_T, VB_T, True, _phase_plan(seq, (1,), (1.0,)),
                                  sink=sink_tiles, sample=(hq_s, ca_t, cb_t, sink_col, units * hosts_a, 1))
    oa_s = jnp.concatenate([oa_s1, oa_s2], axis=0)
    ob_s = jnp.concatenate([ob_s1, ob_s2], axis=0)
    y = _output(x2, oa.reshape(b * seq, A_W), ob.reshape(b * seq, B_W), sg, wo, g2, b2, alpha, 2 * tm)
    kv_a = _kv_rows(kva_t, A_HEADS)
    kv_b = _kv_rows(kvb_t, B_KV_HEADS)

    y_s = _output(xs2, oa_s.reshape(nb, A_W), ob_s.reshape(nb, B_W), sg_s, wo, g2, b2, alpha, nb)
    new_a = _kv_rows(new_a_t, A_HEADS).reshape(nb, 1, 2, A_HEADS, HEAD_DIM)
    new_b = _kv_rows(new_b_t, B_KV_HEADS).reshape(nb, 1, 2, B_KV_HEADS, HEAD_DIM)
    return (y.reshape(b, seq, D_MODEL), y_s.reshape(nb, 1, D_MODEL), kv_a, kv_b, new_a, new_b)


def kernel(x_prompt, x_sample, cache_a_kv, cache_b_kv, w_in, attn_sinks, w_o, ln_g, ln_b):
    depth = w_in.shape[0]
    assert depth == 1 and x_sample.shape[1] == 1, "single layer, one sample token per sequence"
    assert cache_a_kv.shape[2] == BAND * max(DILATIONS), "mixer-A cache covers every strided read"
    assert cache_b_kv.shape[2] == B_WINDOW
    alpha = (2 * depth) ** 0.25
    outs = _layer(x_prompt, x_sample, cache_a_kv[0], cache_b_kv[0], w_in[0], attn_sinks[0],
                  w_o[0], ln_g[0], ln_b[0], alpha, PAST_LEN)
    yp, ys, kv_a, kv_b, new_a, new_b = outs
    return (yp, ys, kv_a[None], kv_b[None], new_a[None], new_b[None])
```

```python
import functools

import jax
import jax.numpy as jnp
import numpy as np
from jax import lax
from jax.experimental import pallas as pl
from jax.experimental.pallas import tpu as pltpu

D_MODEL = 1024
HEAD_DIM = 64
A_HEADS = 8
B_HEADS = 8
B_KV_HEADS = 2
DILATIONS = (1, 4, 16)
BAND = 128
A_WINDOW = 2048
B_WINDOW = 128
PAST_LEN = 16384
ROT_DIM = HEAD_DIM // 4
ROPE_THETA = 500000.0
LN_EPS = 1e-5
A_W = A_HEADS * HEAD_DIM
B_W = B_HEADS * HEAD_DIM
B_KV_W = B_KV_HEADS * HEAD_DIM
LANES = 128
HQ_W = 3 * A_W + B_W + 2 * B_KV_W
HB_W = B_W + 2 * B_KV_W
G_W = A_W + B_W
IN_WIDTH = HQ_W + G_W
QA_T, KA_T, VA_T = 0, A_W // LANES, 2 * A_W // LANES
QB_T = 3 * A_W // LANES
KB_T = QB_T + B_W // LANES
VB_T = KB_T + 1
VMEM_LIMIT = 56 * 1024 * 1024
COL_CHUNK = 256
NEG_INF = float("-inf")
LOG2E = 1.4426950408889634
BLOCK_COST = {1: 1.0, 4: 1.2, 16: 1.55}
PHASE_SHARE = (0.29, 0.236, 0.237, 0.237)
RUN_BLOCKS = 8


def _proj_kernel(x_ref, w_ref, cos_ref, sinp_ref, sinm_ref, hq_ref, sg_ref, kva_ref, kvb_ref, hb_ref,
                 *, wb):
    tm = x_ref.shape[0]
    x = x_ref[...].astype(jnp.bfloat16)
    cos = cos_ref[...]
    sinp = sinp_ref[...]
    sinm = sinm_ref[...]
    rope_tiles = set(range(QA_T, VA_T)) | set(range(QB_T, VB_T))
    for c in range(IN_WIDTH // COL_CHUNK):
        acc = jnp.dot(x, w_ref[:, c * COL_CHUNK:(c + 1) * COL_CHUNK],
                      preferred_element_type=jnp.float32)
        for half in range(COL_CHUNK // LANES):
            tile = c * (COL_CHUNK // LANES) + half
            t = acc[:, half * LANES:(half + 1) * LANES]
            if tile < HQ_W // LANES:
                if tile in rope_tiles:
                    t = (t * cos + pltpu.roll(t, ROT_DIM // 2, 1) * sinp
                         + pltpu.roll(t, LANES - ROT_DIM // 2, 1) * sinm)
                hq_ref[:, tile * LANES:(tile + 1) * LANES] = t
                if KA_T <= tile < QB_T:
                    kv, j = divmod(tile - KA_T, A_W // LANES)
                    kva_ref[kv, j * LANES:(j + 1) * LANES, :] = t.T
                elif tile >= KB_T:
                    kvb_ref[tile - KB_T] = t[tm - wb:].T
                if tile >= QB_T:
                    hb_ref[:, (tile - QB_T) * LANES:(tile - QB_T + 1) * LANES] = t.astype(hb_ref.dtype)
            else:
                g = tile - HQ_W // LANES
                sg_ref[:, g * LANES:(g + 1) * LANES] = (t * jax.nn.sigmoid(t)).astype(sg_ref.dtype)


def _project(x2, w, cos, sinp, sinm, tm, seq, wa, wb):
    n = x2.shape[0]
    tab_blocks = cos.shape[0] // tm
    tps = seq // tm
    first_a = tps - wa // tm
    assert seq % tm == 0 and wa % tm == 0 and wb <= tm and cos.shape[0] == seq
    row = lambda i: (i, 0)
    tab = lambda i: (i % tab_blocks, 0)
    return pl.pallas_call(
        functools.partial(_proj_kernel, wb=wb),
        grid=(n // tm,),
        in_specs=[
            pl.BlockSpec((tm, D_MODEL), row),
            pl.BlockSpec((D_MODEL, IN_WIDTH), lambda i: (0, 0)),
            pl.BlockSpec((tm, LANES), tab),
            pl.BlockSpec((tm, LANES), tab),
            pl.BlockSpec((tm, LANES), tab),
        ],
        out_specs=[
            pl.BlockSpec((tm, HQ_W), row),
            pl.BlockSpec((tm, G_W), row),
            pl.BlockSpec((None, 2, A_W, tm),
                         lambda i: (i // tps, 0, 0, jnp.maximum(i % tps - first_a, 0))),
            pl.BlockSpec((None, 2, B_KV_W, wb), lambda i: (i // tps, 0, 0, 0)),
            pl.BlockSpec((tm, HB_W), row),
        ],
        out_shape=[jax.ShapeDtypeStruct((n, HQ_W), jnp.float32),
                   jax.ShapeDtypeStruct((n, G_W), jnp.bfloat16),
                   jax.ShapeDtypeStruct((n // seq, 2, A_W, wa), jnp.float32),
                   jax.ShapeDtypeStruct((n // seq, 2, B_KV_W, wb), jnp.float32),
                   jax.ShapeDtypeStruct((n, HB_W), jnp.bfloat16)],
        compiler_params=pltpu.CompilerParams(
            dimension_semantics=("arbitrary",), vmem_limit_bytes=VMEM_LIMIT),
        name="proj",
    )(x2, w, cos, sinp, sinm)


def _lane_lo(shape):
    return lax.broadcasted_iota(jnp.int32, shape, len(shape) - 1) < HEAD_DIM


def _sample_step(hq_ref, ca_ref, cb_ref, w_ref, sink_ref, oa_ref, ob_ref):
    row = hq_ref[...]
    qa = row[:, QA_T * LANES:KA_T * LANES]
    ka_new = row[:, KA_T * LANES:VA_T * LANES]
    va_new = row[:, VA_T * LANES:QB_T * LANES]
    qb = row[:, QB_T * LANES:KB_T * LANES]
    kb_new = row[:, KB_T * LANES:VB_T * LANES]
    vb_new = row[:, VB_T * LANES:HQ_W]
    buf_a = ca_ref.shape[-1]

    hrow = lax.broadcasted_iota(jnp.int32, (A_HEADS, A_W), 0)
    hcol = lax.broadcasted_iota(jnp.int32, (A_HEADS, A_W), 1) // HEAD_DIM
    own = hrow == hcol

    def spread(col):
        return jnp.sum(jnp.where(own, jnp.broadcast_to(col, (A_HEADS, A_W)), 0.0), axis=0, keepdims=True)

    q_bd = jnp.where(own, jnp.broadcast_to(qa, (A_HEADS, A_W)), 0.0)
    s_new = jnp.sum(q_bd * ka_new, axis=1, keepdims=True)
    w = w_ref[...]
    s = jnp.dot(q_bd.astype(jnp.bfloat16), ca_ref[0].astype(jnp.bfloat16),
                preferred_element_type=jnp.float32)
    s = jnp.where(w > 0.0, s, NEG_INF)
    m = jnp.maximum(jnp.max(s, axis=1, keepdims=True), s_new)
    p = jnp.exp2(s - m) * w
    p_new = float(len(DILATIONS)) * jnp.exp2(s_new - m)
    den = jnp.sum(p, axis=1, keepdims=True) + p_new
    folded = []
    for h in range(A_HEADS):
        acc = None
        for t in range(buf_a // LANES):
            cols = slice(t * LANES, (t + 1) * LANES)
            part = ca_ref[1, h * HEAD_DIM:(h + 1) * HEAD_DIM, cols] * p[h:h + 1, cols]
            acc = part if acc is None else acc + part
        folded.append(acc)
    o = jnp.sum(jnp.concatenate(folded, axis=0).T, axis=0, keepdims=True)
    oa_ref[...] = (o + spread(p_new) * va_new) / spread(den)

    prow = lax.broadcasted_iota(jnp.int32, (B_HEADS, LANES), 0)
    pcol = lax.broadcasted_iota(jnp.int32, (B_HEADS, LANES), 1) // HEAD_DIM
    qb_rows = jnp.zeros((B_HEADS, LANES), jnp.float32)
    for t in range(B_W // LANES):
        qb_rows = jnp.where(prow // 2 == t,
                            jnp.broadcast_to(qb[:, t * LANES:(t + 1) * LANES], (B_HEADS, LANES)), qb_rows)
    own_b = (prow % 2) == pcol
    qb_bd = jnp.where(own_b, qb_rows, 0.0)
    sb = jnp.dot(qb_bd.astype(jnp.bfloat16), cb_ref[0].astype(jnp.bfloat16),
                 preferred_element_type=jnp.float32)
    sb_new = jnp.sum(qb_bd * kb_new, axis=1, keepdims=True)
    sink = sink_ref[...]
    mb = jnp.maximum(jnp.maximum(jnp.max(sb, axis=1, keepdims=True), sb_new), sink)
    pb = jnp.exp2(sb - mb)
    pb_new = jnp.exp2(sb_new - mb)
    den_b = jnp.sum(pb, axis=1, keepdims=True) + pb_new + jnp.exp2(sink - mb)
    ob = lax.dot_general(pb.astype(jnp.bfloat16), cb_ref[1].astype(jnp.bfloat16),
                         (((1,), (1,)), ((), ())), preferred_element_type=jnp.float32)
    ob = (ob + pb_new * vb_new) / den_b
    lo = _lane_lo((1, LANES))
    for t in range(B_W // LANES):
        ob_ref[:, t * LANES:(t + 1) * LANES] = jnp.where(lo, ob[2 * t:2 * t + 1], ob[2 * t + 1:2 * t + 2])


def _sample_weights(buf):
    dist = buf - np.arange(buf)
    w = np.zeros((buf,), np.float32)
    for d in DILATIONS:
        w += ((dist % d == 0) & (dist <= BAND * d)).astype(np.float32)
    return jnp.asarray(w[None, :])


def _attn_kernel(*refs, phases, seq, with_sink, host_phases):
    refs = list(refs)
    q_ref, k_ref, v_ref, bias_ref = refs[:4]
    del refs[:4]
    sink_ref = refs.pop(0) if with_sink else None
    sample_in = [refs.pop(0) for _ in range(5)] if host_phases else None
    o_ref = refs.pop(0)
    sample_out = [refs.pop(0) for _ in range(2)] if host_phases else None
    dils = [d for phase in phases for d, _, _, _ in phase]
    multi = len(set(dils)) > 1
    if multi:
        acc_ref, m_ref, l_ref = refs
    lo = _lane_lo((BAND, LANES))

    def pair(a, b):
        return jnp.where(lo, a, b)

    head_lanes = (jnp.where(lo, 1.0, 0.0).astype(jnp.bfloat16),
                  jnp.where(lo, 0.0, 1.0).astype(jnp.bfloat16))

    def by_head(a):
        return a * head_lanes[0], a * head_lanes[1]

    ones = tuple(jnp.concatenate([t, t], axis=0) for t in head_lanes)

    def run(d, r, i_lo, i_hi):
        first = d == dils[0]
        last = d == dils[-1]

        def rows(start):
            return pl.ds(start, BAND) if d == 1 else pl.ds(start, BAND, stride=d)

        def block(i, carry):
            kp, vp = carry
            start = r + i * (BAND * d)
            q = q_ref[rows(start), :]
            kc = k_ref[rows(start), :].astype(jnp.bfloat16)
            vc = by_head(v_ref[rows(start), :].astype(jnp.bfloat16))
            if q.dtype == jnp.bfloat16:
                q2 = jnp.concatenate(by_head(q), axis=0)
            else:
                q2 = jnp.concatenate([jnp.where(lo, q, 0.0), jnp.where(lo, 0.0, q)],
                                     axis=0).astype(jnp.bfloat16)
            kk = jnp.concatenate([kp, kc], axis=0)
            s = lax.dot_general(q2, kk, (((1,), (1,)), ((), ())),
                                preferred_element_type=jnp.float32)
            s = s + bias_ref[min(i, 1)]
            m = jnp.max(s, axis=1, keepdims=True)
            p = jnp.exp2(s - m).astype(jnp.bfloat16)
            vv = jnp.concatenate(
                [jnp.concatenate([jnp.concatenate([vp[h], vc[h]], axis=0), ones[h]], axis=1)
                 for h in range(2)], axis=0)
            pv = jnp.dot(jnp.concatenate([p[:BAND], p[BAND:]], axis=1), vv,
                         preferred_element_type=jnp.float32)
            o_n = pv[:, :LANES]
            l_n = pv[:, LANES:]
            m_n = pair(jnp.broadcast_to(m[:BAND], (BAND, LANES)),
                       jnp.broadcast_to(m[BAND:], (BAND, LANES)))
            if multi and not first:
                m_o = m_ref[rows(start), :]
                m_t = jnp.maximum(m_o, m_n)
                a_o = jnp.exp2(m_o - m_t)
                a_n = jnp.exp2(m_n - m_t)
                o_n = acc_ref[rows(start), :] * a_o + o_n * a_n
                l_n = l_ref[rows(start), :] * a_o + l_n * a_n
                m_n = m_t
            if last:
                if with_sink:
                    sink = sink_ref[...]
                    m_t = jnp.maximum(m_n, sink)
                    a_n = jnp.exp2(m_n - m_t)
                    o_n = o_n * a_n
                    l_n = l_n * a_n + jnp.exp2(sink - m_t)
                o_ref[rows(start), :] = (o_n / l_n).astype(o_ref.dtype)
            else:
                acc_ref[rows(start), :] = o_n
                l_ref[rows(start), :] = l_n
                m_ref[rows(start), :] = m_n
            return kc, vc

        if i_lo == 0:
            zero = jnp.zeros((BAND, LANES), jnp.bfloat16)
            carry = (zero, (zero, zero))
        else:
            start = r + (i_lo - 1) * (BAND * d)
            carry = (k_ref[rows(start), :].astype(jnp.bfloat16),
                     by_head(v_ref[rows(start), :].astype(jnp.bfloat16)))
        for i in range(i_lo, i_hi):
            carry = block(i, carry)

    def run_phase(ph):
        if ph in host_phases:
            _sample_step(*sample_in, *sample_out)
        for seg in phases[ph]:
            run(*seg)

    if len(phases) == 1:
        run_phase(0)
    else:
        for ph in range(len(phases)):
            pl.when(pl.program_id(2) == ph)(functools.partial(run_phase, ph))


def _band_bias():
    qi = np.arange(BAND)[:, None]
    ki = np.arange(2 * BAND)[None, :]
    dist = BAND + qi - ki
    band = (dist >= 0) & (dist <= BAND)
    full = np.where(band, 0.0, NEG_INF).astype(np.float32)
    head = np.where(band & (ki >= BAND), 0.0, NEG_INF).astype(np.float32)
    both = np.stack([head, full])
    return jnp.asarray(np.concatenate([both, both], axis=1))


def _phase_plan(seq, dilations, shares):
    runs = [(d, r, lo, min(lo + RUN_BLOCKS, seq // d // BAND))
            for d in sorted(dilations, reverse=True) for r in range(d)
            for lo in range(0, seq // d // BAND, RUN_BLOCKS)]
    cost = lambda run: BLOCK_COST[run[0]] * (run[3] - run[2])
    bounds = np.cumsum(shares) * sum(cost(run) for run in runs)
    phases, cur, spent = [], [], 0.0
    for run in runs:
        cur.append(run)
        spent += cost(run)
        if len(phases) < len(shares) - 1 and spent >= bounds[len(phases)]:
            phases.append(tuple(cur))
            cur = []
    phases.append(tuple(cur))
    assert len(phases) == len(shares) and all(phases)
    return tuple(phases)


def _attention(hq3, q_tile, k_tile, v_tile, shared_kv, phases, sink=None, sample=None):
    b, seq, _ = hq3.shape
    n_pairs = A_W // LANES
    n_ph = len(phases)
    with_sink = sink is not None
    hosts = sample[5] if sample is not None else 0
    host_phases = tuple(range(n_ph - hosts, n_ph))
    kv_off = (lambda u: 0) if shared_kv else (lambda u: u)
    in_specs = [
        pl.BlockSpec((None, seq, LANES), lambda i, u, j: (i, 0, q_tile + u)),
        pl.BlockSpec((None, seq, LANES), lambda i, u, j: (i, 0, k_tile + kv_off(u))),
        pl.BlockSpec((None, seq, LANES), lambda i, u, j: (i, 0, v_tile + kv_off(u))),
        pl.BlockSpec((2, 2 * BAND, 2 * BAND), lambda i, u, j: (0, 0, 0)),
    ]
    args = [hq3, hq3, hq3, _band_bias()]
    if with_sink:
        in_specs.append(pl.BlockSpec((None, 1, LANES), lambda i, u, j: (u, 0, 0)))
        args.append(sink)
    out_specs = [pl.BlockSpec((None, seq, LANES), lambda i, u, j: (i, 0, u))]
    out_shape = [jax.ShapeDtypeStruct((b, seq, n_pairs * LANES), jnp.bfloat16)]
    if hosts:
        hq_s, cache_a_t, cache_b_t, sink_col, first, _ = sample
        n = b * n_pairs * hosts
        assert 0 < hosts <= n_ph and first + n <= hq_s.shape[0]
        buf_a = cache_a_t.shape[-1]
        buf_b = cache_b_t.shape[-1]

        def local(i, u, j):
            unit = i * n_pairs + u
            return jnp.maximum(unit * hosts + jnp.maximum(j - (n_ph - hosts), -1), 0)

        in_specs += [
            pl.BlockSpec((None, 1, HQ_W), lambda i, u, j: (first + local(i, u, j), 0, 0)),
            pl.BlockSpec((None, 2, A_W, buf_a), lambda i, u, j: (first + local(i, u, j), 0, 0, 0)),
            pl.BlockSpec((None, 2, B_KV_W, buf_b), lambda i, u, j: (first + local(i, u, j), 0, 0, 0)),
            pl.BlockSpec((1, buf_a), lambda i, u, j: (0, 0)),
            pl.BlockSpec((B_HEADS, 1), lambda i, u, j: (0, 0)),
        ]
        args += [hq_s.reshape(hq_s.shape[0], 1, HQ_W), cache_a_t, cache_b_t,
                 _sample_weights(buf_a), sink_col]
        out_specs += [pl.BlockSpec((None, 1, A_W), lambda i, u, j: (local(i, u, j), 0, 0)),
                      pl.BlockSpec((None, 1, B_W), lambda i, u, j: (local(i, u, j), 0, 0))]
        out_shape += [jax.ShapeDtypeStruct((n, 1, A_W), jnp.float32),
                      jax.ShapeDtypeStruct((n, 1, B_W), jnp.float32)]
    scratch = []
    if len({d for phase in phases for d, _, _, _ in phase}) > 1:
        scratch = [pltpu.VMEM((seq, LANES), jnp.float32)] * 3
    return pl.pallas_call(
        functools.partial(_attn_kernel, phases=phases, seq=seq, with_sink=with_sink,
                          host_phases=host_phases),
        grid=(b, n_pairs, n_ph),
        in_specs=in_specs,
        out_specs=out_specs,
        out_shape=out_shape,
        scratch_shapes=scratch,
        compiler_params=pltpu.CompilerParams(
            dimension_semantics=("arbitrary", "arbitrary", "arbitrary"),
            vmem_limit_bytes=VMEM_LIMIT),
        name="attn_b" if with_sink else "attn_a",
    )(*args)


def _out_kernel(x_ref, oa_ref, ob_ref, sg_ref, wo_ref, g_ref, b_ref, y_ref, *, alpha):
    sg = sg_ref[...].astype(jnp.float32)
    mix_a = (oa_ref[...].astype(jnp.float32) * sg[:, :A_W]).astype(jnp.bfloat16)
    mix_b = (ob_ref[...].astype(jnp.float32) * sg[:, A_W:]).astype(jnp.bfloat16)
    out = (jnp.dot(mix_a, wo_ref[:A_W, :], preferred_element_type=jnp.float32)
           + jnp.dot(mix_b, wo_ref[A_W:, :], preferred_element_type=jnp.float32))
    z = alpha * x_ref[...] + out
    mu = jnp.mean(z, axis=1, keepdims=True)
    zc = z - mu
    var = jnp.mean(zc * zc, axis=1, keepdims=True)
    y_ref[...] = zc * lax.rsqrt(var + LN_EPS) * g_ref[...] + b_ref[...]


def _output(x2, oa, ob, sg, wo, g, b, alpha, tm):
    n = x2.shape[0]
    row = lambda i: (i, 0)
    fixed = lambda i: (0, 0)
    return pl.pallas_call(
        functools.partial(_out_kernel, alpha=alpha),
        grid=(n // tm,),
        in_specs=[
            pl.BlockSpec((tm, D_MODEL), row),
            pl.BlockSpec((tm, A_W), row),
            pl.BlockSpec((tm, B_W), row),
            pl.BlockSpec((tm, G_W), row),
            pl.BlockSpec((G_W, D_MODEL), fixed),
            pl.BlockSpec((1, D_MODEL), fixed),
            pl.BlockSpec((1, D_MODEL), fixed),
        ],
        out_specs=pl.BlockSpec((tm, D_MODEL), row),
        out_shape=jax.ShapeDtypeStruct((n, D_MODEL), jnp.float32),
        compiler_params=pltpu.CompilerParams(
            dimension_semantics=("arbitrary",), vmem_limit_bytes=VMEM_LIMIT),
        name="out_proj",
    )(x2, oa, ob, sg, wo, g, b)


def _rope_tables(pos):
    inv = ROPE_THETA ** (-np.arange(0, ROT_DIM, 2, dtype=np.float64) / ROT_DIM)
    ang = pos.astype(np.float64)[:, None] * inv[None, :]
    cos = np.cos(ang)
    sin = np.sin(ang)
    n = pos.shape[0]
    half = ROT_DIM // 2
    pad = HEAD_DIM - ROT_DIM
    one = np.ones((n, pad))
    zero = np.zeros((n, pad))
    zh = np.zeros((n, half))
    cos_h = np.concatenate([cos, cos, one], axis=1)
    sinp_h = np.concatenate([zh, sin, zero], axis=1)
    sinm_h = np.concatenate([-sin, zh, zero], axis=1)
    rep = LANES // HEAD_DIM
    return tuple(jnp.asarray(np.tile(t, (1, rep)), jnp.float32) for t in (cos_h, sinp_h, sinm_h))


def _pair_b_heads(a, axis):
    group = B_HEADS // B_KV_HEADS
    shape = a.shape
    split = shape[:axis] + (B_KV_HEADS, group, shape[axis] // B_HEADS) + shape[axis + 1:]
    return jnp.swapaxes(a.reshape(split), axis, axis + 1).reshape(shape)


def _kv_rows(kvt, heads):
    n, _, _, rows = kvt.shape
    return jnp.transpose(kvt.reshape(n, 2, heads, HEAD_DIM, rows), (0, 4, 1, 2, 3))


def _layer(xp, xs, cache_a, cache_b, w_in, sinks, w_o, ln_g, ln_b, alpha, past_len):
    b, seq, _ = xp.shape
    nb = xs.shape[0]
    qa, ka, va, ga, qb, kb, vb, gb = jnp.split(
        w_in, np.cumsum([A_W, A_W, A_W, A_W, B_W, B_KV_W, B_KV_W])[:].tolist(), axis=1)
    q_scale = HEAD_DIM ** -0.5 * LOG2E
    w = jnp.concatenate([qa * q_scale, ka, va, _pair_b_heads(qb, 1) * q_scale, kb, vb,
                         ga, _pair_b_heads(gb, 1)], axis=1).astype(jnp.bfloat16)
    wo = jnp.concatenate([w_o[:A_W], _pair_b_heads(w_o[A_W:], 0)], axis=0).astype(jnp.bfloat16)
    sink_p = _pair_b_heads(sinks.astype(jnp.float32) * LOG2E, 0)
    sink_tiles = jnp.repeat(sink_p, HEAD_DIM).reshape(B_W // LANES, 1, LANES)
    g2 = ln_g.reshape(1, D_MODEL)
    b2 = ln_b.reshape(1, D_MODEL)

    tm = 512
    wa = min(A_WINDOW, seq)
    wb = min(B_WINDOW, seq)
    x2 = xp.reshape(b * seq, D_MODEL)
    hq, sg, kva_t, kvb_t, hb = _project(x2, w, *_rope_tables(np.arange(seq)), tm, seq, wa, wb)
    hq3 = hq.reshape(b, seq, HQ_W)

    xs2 = xs.reshape(nb, D_MODEL)
    pos_s = np.full((nb,), past_len, np.int64)
    hq_s, sg_s, new_a_t, new_b_t, _ = _project(xs2, w, *_rope_tables(pos_s), nb, nb, nb, nb)
    ca_t = jnp.transpose(cache_a, (0, 2, 3, 4, 1)).reshape(nb, 2, A_W, cache_a.shape[1])
    cb_t = jnp.transpose(cache_b, (0, 2, 3, 4, 1)).reshape(nb, 2, B_KV_W, cache_b.shape[1])

    sink_col = sink_p.reshape(B_HEADS, 1)
    units = b * (A_W // LANES)
    hosts_a = len(PHASE_SHARE) - 1
    assert nb == units * (hosts_a + 1), "every sample sequence has a hosting grid step"
    oa, oa_s1, ob_s1 = _attention(hq3, QA_T, KA_T, VA_T, False, _phase_plan(seq, DILATIONS, PHASE_SHARE),
                                  sample=(hq_s, ca_t, cb_t, sink_col, 0, hosts_a))
    ob, oa_s2, ob_s2 = _attention(hb.reshape(b, seq, HB_W), 0, KB_T - QB_T, VB_T - QB_T, True,
                                  _phase_plan(seq, (1,), (1.0,)),
                                  sink=sink_tiles, sample=(hq_s, ca_t, cb_t, sink_col, units * hosts_a, 1))
    oa_s = jnp.concatenate([oa_s1, oa_s2], axis=0)
    ob_s = jnp.concatenate([ob_s1, ob_s2], axis=0)
    y = _output(x2, oa.reshape(b * seq, A_W), ob.reshape(b * seq, B_W), sg, wo, g2, b2, alpha, 2 * tm)
    kv_a = _kv_rows(kva_t, A_HEADS)
    kv_b = _kv_rows(kvb_t, B_KV_HEADS)

    y_s = _output(xs2, oa_s.reshape(nb, A_W), ob_s.reshape(nb, B_W), sg_s, wo, g2, b2, alpha, nb)
    new_a = _kv_rows(new_a_t, A_HEADS).reshape(nb, 1, 2, A_HEADS, HEAD_DIM)
    new_b = _kv_rows(new_b_t, B_KV_HEADS).reshape(nb, 1, 2, B_KV_HEADS, HEAD_DIM)
    return (y.reshape(b, seq, D_MODEL), y_s.reshape(nb, 1, D_MODEL), kv_a, kv_b, new_a, new_b)


def kernel(x_prompt, x_sample, cache_a_kv, cache_b_kv, w_in, attn_sinks, w_o, ln_g, ln_b):
    depth = w_in.shape[0]
    assert depth == 1 and x_sample.shape[1] == 1, "single layer, one sample token per sequence"
    assert cache_a_kv.shape[2] == BAND * max(DILATIONS), "mixer-A cache covers every strided read"
    assert cache_b_kv.shape[2] == B_WINDOW
    alpha = (2 * depth) ** 0.25
    outs = _layer(x_prompt, x_sample, cache_a_kv[0], cache_b_kv[0], w_in[0], attn_sinks[0],
                  w_o[0], ln_g[0], ln_b[0], alpha, PAST_LEN)
    yp, ys, kv_a, kv_b, new_a, new_b = outs
    return (yp, ys, kv_a[None], kv_b[None], new_a[None], new_b[None])
```

```python
import functools

import jax
import jax.numpy as jnp
import numpy as np
from jax import lax
from jax.experimental import pallas as pl
from jax.experimental.pallas import tpu as pltpu

D_MODEL = 1024
HEAD_DIM = 64
A_HEADS = 8
B_HEADS = 8
B_KV_HEADS = 2
DILATIONS = (1, 4, 16)
BAND = 128
A_WINDOW = 2048
B_WINDOW = 128
PAST_LEN = 16384
ROT_DIM = HEAD_DIM // 4
ROPE_THETA = 500000.0
LN_EPS = 1e-5
A_W = A_HEADS * HEAD_DIM
B_W = B_HEADS * HEAD_DIM
B_KV_W = B_KV_HEADS * HEAD_DIM
LANES = 128
HQ_W = 3 * A_W + B_W + 2 * B_KV_W
HB_W = B_W + 2 * B_KV_W
G_W = A_W + B_W
IN_WIDTH = HQ_W + G_W
QA_T, KA_T, VA_T = 0, A_W // LANES, 2 * A_W // LANES
QB_T = 3 * A_W // LANES
KB_T = QB_T + B_W // LANES
VB_T = KB_T + 1
VMEM_LIMIT = 56 * 1024 * 1024
COL_CHUNK = 256
NEG_INF = float("-inf")
LOG2E = 1.4426950408889634
BLOCK_COST = {1: 1.0, 4: 1.2, 16: 1.55}
PHASE_SHARE = (0.29, 0.236, 0.237, 0.237)
RUN_BLOCKS = 8


def _proj_kernel(x_ref, w_ref, cos_ref, sinp_ref, sinm_ref, hq_ref, sg_ref, kva_ref, kvb_ref, hb_ref,
                 *, wb):
    tm = x_ref.shape[0]
    x = x_ref[...].astype(jnp.bfloat16)
    cos = cos_ref[...]
    sinp = sinp_ref[...]
    sinm = sinm_ref[...]
    rope_tiles = set(range(QA_T, VA_T)) | set(range(QB_T, VB_T))
    for c in range(IN_WIDTH // COL_CHUNK):
        acc = jnp.dot(x, w_ref[:, c * COL_CHUNK:(c + 1) * COL_CHUNK],
                      preferred_element_type=jnp.float32)
        for half in range(COL_CHUNK // LANES):
            tile = c * (COL_CHUNK // LANES) + half
            t = acc[:, half * LANES:(half + 1) * LANES]
            if tile < HQ_W // LANES:
                if tile in rope_tiles:
                    t = (t * cos + pltpu.roll(t, ROT_DIM // 2, 1) * sinp
                         + pltpu.roll(t, LANES - ROT_DIM // 2, 1) * sinm)
                hq_ref[:, tile * LANES:(tile + 1) * LANES] = t
                if KA_T <= tile < QB_T:
                    kv, j = divmod(tile - KA_T, A_W // LANES)
                    kva_ref[kv, j * LANES:(j + 1) * LANES, :] = t.T
                elif tile >= KB_T:
                    kvb_ref[tile - KB_T] = t[tm - wb:].T
                if tile >= QB_T:
                    hb_ref[:, (tile - QB_T) * LANES:(tile - QB_T + 1) * LANES] = t.astype(hb_ref.dtype)
            else:
                g = tile - HQ_W // LANES
                sg_ref[:, g * LANES:(g + 1) * LANES] = (t * jax.nn.sigmoid(t)).astype(sg_ref.dtype)


def _project(x2, w, cos, sinp, sinm, tm, seq, wa, wb):
    n = x2.shape[0]
    tab_blocks = cos.shape[0] // tm
    tps = seq // tm
    first_a = tps - wa // tm
    assert seq % tm == 0 and wa % tm == 0 and wb <= tm and cos.shape[0] == seq
    row = lambda i: (i, 0)
    tab = lambda i: (i % tab_blocks, 0)
    return pl.pallas_call(
        functools.partial(_proj_kernel, wb=wb),
        grid=(n // tm,),
        in_specs=[
            pl.BlockSpec((tm, D_MODEL), row),
            pl.BlockSpec((D_MODEL, IN_WIDTH), lambda i: (0, 0)),
            pl.BlockSpec((tm, LANES), tab),
            pl.BlockSpec((tm, LANES), tab),
            pl.BlockSpec((tm, LANES), tab),
        ],
        out_specs=[
            pl.BlockSpec((tm, HQ_W), row),
            pl.BlockSpec((tm, G_W), row),
            pl.BlockSpec((None, 2, A_W, tm),
                         lambda i: (i // tps, 0, 0, jnp.maximum(i % tps - first_a, 0))),
            pl.BlockSpec((None, 2, B_KV_W, wb), lambda i: (i // tps, 0, 0, 0)),
            pl.BlockSpec((tm, HB_W), row),
        ],
        out_shape=[jax.ShapeDtypeStruct((n, HQ_W), jnp.float32),
                   jax.ShapeDtypeStruct((n, G_W), jnp.bfloat16),
                   jax.ShapeDtypeStruct((n // seq, 2, A_W, wa), jnp.float32),
                   jax.ShapeDtypeStruct((n // seq, 2, B_KV_W, wb), jnp.float32),
                   jax.ShapeDtypeStruct((n, HB_W), jnp.bfloat16)],
        compiler_params=pltpu.CompilerParams(
            dimension_semantics=("arbitrary",), vmem_limit_bytes=VMEM_LIMIT),
        name="proj",
    )(x2, w, cos, sinp, sinm)


def _lane_lo(shape):
    return lax.broadcasted_iota(jnp.int32, shape, len(shape) - 1) < HEAD_DIM


def _sample_step(hq_ref, ca_ref, cb_ref, w_ref, sink_ref, oa_ref, ob_ref):
    row = hq_ref[...]
    qa = row[:, QA_T * LANES:KA_T * LANES]
    ka_new = row[:, KA_T * LANES:VA_T * LANES]
    va_new = row[:, VA_T * LANES:QB_T * LANES]
    qb = row[:, QB_T * LANES:KB_T * LANES]
    kb_new = row[:, KB_T * LANES:VB_T * LANES]
    vb_new = row[:, VB_T * LANES:HQ_W]
    buf_a = ca_ref.shape[-1]

    hrow = lax.broadcasted_iota(jnp.int32, (A_HEADS, A_W), 0)
    hcol = lax.broadcasted_iota(jnp.int32, (A_HEADS, A_W), 1) // HEAD_DIM
    own = hrow == hcol

    def spread(col):
        return jnp.sum(jnp.where(own, jnp.broadcast_to(col, (A_HEADS, A_W)), 0.0), axis=0, keepdims=True)

    q_bd = jnp.where(own, jnp.broadcast_to(qa, (A_HEADS, A_W)), 0.0)
    s_new = jnp.sum(q_bd * ka_new, axis=1, keepdims=True)
    w = w_ref[...]
    s = jnp.dot(q_bd.astype(jnp.bfloat16), ca_ref[0].astype(jnp.bfloat16),
                preferred_element_type=jnp.float32)
    s = jnp.where(w > 0.0, s, NEG_INF)
    m = jnp.maximum(jnp.max(s, axis=1, keepdims=True), s_new)
    p = jnp.exp2(s - m) * w
    p_new = float(len(DILATIONS)) * jnp.exp2(s_new - m)
    den = jnp.sum(p, axis=1, keepdims=True) + p_new
    folded = []
    for h in range(A_HEADS):
        acc = None
        for t in range(buf_a // LANES):
            cols = slice(t * LANES, (t + 1) * LANES)
            part = ca_ref[1, h * HEAD_DIM:(h + 1) * HEAD_DIM, cols] * p[h:h + 1, cols]
            acc = part if acc is None else acc + part
        folded.append(acc)
    o = jnp.sum(jnp.concatenate(folded, axis=0).T, axis=0, keepdims=True)
    oa_ref[...] = (o + spread(p_new) * va_new) / spread(den)

    prow = lax.broadcasted_iota(jnp.int32, (B_HEADS, LANES), 0)
    pcol = lax.broadcasted_iota(jnp.int32, (B_HEADS, LANES), 1) // HEAD_DIM
    qb_rows = jnp.zeros((B_HEADS, LANES), jnp.float32)
    for t in range(B_W // LANES):
        qb_rows = jnp.where(prow // 2 == t,
                            jnp.broadcast_to(qb[:, t * LANES:(t + 1) * LANES], (B_HEADS, LANES)), qb_rows)
    own_b = (prow % 2) == pcol
    qb_bd = jnp.where(own_b, qb_rows, 0.0)
    sb = jnp.dot(qb_bd.astype(jnp.bfloat16), cb_ref[0].astype(jnp.bfloat16),
                 preferred_element_type=jnp.float32)
    sb_new = jnp.sum(qb_bd * kb_new, axis=1, keepdims=True)
    sink = sink_ref[...]
    mb = jnp.maximum(jnp.maximum(jnp.max(sb, axis=1, keepdims=True), sb_new), sink)
    pb = jnp.exp2(sb - mb)
    pb_new = jnp.exp2(sb_new - mb)
    den_b = jnp.sum(pb, axis=1, keepdims=True) + pb_new + jnp.exp2(sink - mb)
    ob = lax.dot_general(pb.astype(jnp.bfloat16), cb_ref[1].astype(jnp.bfloat16),
                         (((1,), (1,)), ((), ())), preferred_element_type=jnp.float32)
    ob = (ob + pb_new * vb_new) / den_b
    lo = _lane_lo((1, LANES))
    for t in range(B_W // LANES):
        ob_ref[:, t * LANES:(t + 1) * LANES] = jnp.where(lo, ob[2 * t:2 * t + 1], ob[2 * t + 1:2 * t + 2])


def _sample_weights(buf):
    dist = buf - np.arange(buf)
    w = np.zeros((buf,), np.float32)
    for d in DILATIONS:
        w += ((dist % d == 0) & (dist <= BAND * d)).astype(np.float32)
    return jnp.asarray(w[None, :])


def _attn_kernel(qa_ref, ka_ref, va_ref, bias_ref, qb_ref, kb_ref, vb_ref, sink_ref,
                 hq_s_ref, ca_ref, cb_ref, w_ref, sink_col_ref,
                 oa_ref, ob_ref, oa_s_ref, ob_s_ref, acc_ref, m_ref, l_ref, *, phases, host_phases):
    sample_refs = (hq_s_ref, ca_ref, cb_ref, w_ref, sink_col_ref, oa_s_ref, ob_s_ref)
    dils = [d for phase in phases for mixer, d, _, _, _ in phase if mixer == "a"]
    lo = _lane_lo((BAND, LANES))

    def pair(a, b):
        return jnp.where(lo, a, b)

    head_lanes = (jnp.where(lo, 1.0, 0.0).astype(jnp.bfloat16),
                  jnp.where(lo, 0.0, 1.0).astype(jnp.bfloat16))

    def by_head(a):
        return a * head_lanes[0], a * head_lanes[1]

    ones = tuple(jnp.concatenate([t, t], axis=0) for t in head_lanes)

    def run(mixer, d, r, i_lo, i_hi):
        if mixer == "a":
            q_ref, k_ref, v_ref, o_ref = qa_ref, ka_ref, va_ref, oa_ref
            multi, with_sink = len(set(dils)) > 1, False
            first, last = d == dils[0], d == dils[-1]
        else:
            q_ref, k_ref, v_ref, o_ref = qb_ref, kb_ref, vb_ref, ob_ref
            multi, with_sink = False, True
            first = last = True

        def rows(start):
            return pl.ds(start, BAND) if d == 1 else pl.ds(start, BAND, stride=d)

        def block(i, carry):
            kp, vp = carry
            start = r + i * (BAND * d)
            q = q_ref[rows(start), :]
            kc = k_ref[rows(start), :].astype(jnp.bfloat16)
            vc = by_head(v_ref[rows(start), :].astype(jnp.bfloat16))
            if q.dtype == jnp.bfloat16:
                q2 = jnp.concatenate(by_head(q), axis=0)
            else:
                q2 = jnp.concatenate([jnp.where(lo, q, 0.0), jnp.where(lo, 0.0, q)],
                                     axis=0).astype(jnp.bfloat16)
            kk = jnp.concatenate([kp, kc], axis=0)
            s = lax.dot_general(q2, kk, (((1,), (1,)), ((), ())),
                                preferred_element_type=jnp.float32)
            s = s + bias_ref[min(i, 1)]
            m = jnp.max(s, axis=1, keepdims=True)
            p = jnp.exp2(s - m).astype(jnp.bfloat16)
            vv = jnp.concatenate(
                [jnp.concatenate([jnp.concatenate([vp[h], vc[h]], axis=0), ones[h]], axis=1)
                 for h in range(2)], axis=0)
            pv = jnp.dot(jnp.concatenate([p[:BAND], p[BAND:]], axis=1), vv,
                         preferred_element_type=jnp.float32)
            o_n = pv[:, :LANES]
            l_n = pv[:, LANES:]
            m_n = pair(jnp.broadcast_to(m[:BAND], (BAND, LANES)),
                       jnp.broadcast_to(m[BAND:], (BAND, LANES)))
            if multi and not first:
                m_o = m_ref[rows(start), :]
                m_t = jnp.maximum(m_o, m_n)
                a_o = jnp.exp2(m_o - m_t)
                a_n = jnp.exp2(m_n - m_t)
                o_n = acc_ref[rows(start), :] * a_o + o_n * a_n
                l_n = l_ref[rows(start), :] * a_o + l_n * a_n
                m_n = m_t
            if last:
                if with_sink:
                    sink = sink_ref[...]
                    m_t = jnp.maximum(m_n, sink)
                    a_n = jnp.exp2(m_n - m_t)
                    o_n = o_n * a_n
                    l_n = l_n * a_n + jnp.exp2(sink - m_t)
                o_ref[rows(start), :] = (o_n / l_n).astype(o_ref.dtype)
            else:
                acc_ref[rows(start), :] = o_n
                l_ref[rows(start), :] = l_n
                m_ref[rows(start), :] = m_n
            return kc, vc

        if i_lo == 0:
            zero = jnp.zeros((BAND, LANES), jnp.bfloat16)
            carry = (zero, (zero, zero))
        else:
            start = r + (i_lo - 1) * (BAND * d)
            carry = (k_ref[rows(start), :].astype(jnp.bfloat16),
                     by_head(v_ref[rows(start), :].astype(jnp.bfloat16)))
        for i in range(i_lo, i_hi):
            carry = block(i, carry)

    def run_phase(ph):
        if ph in host_phases:
            _sample_step(*sample_refs)
        for seg in phases[ph]:
            run(*seg)

    for ph in range(len(phases)):
        pl.when(pl.program_id(2) == ph)(functools.partial(run_phase, ph))


def _band_bias():
    qi = np.arange(BAND)[:, None]
    ki = np.arange(2 * BAND)[None, :]
    dist = BAND + qi - ki
    band = (dist >= 0) & (dist <= BAND)
    full = np.where(band, 0.0, NEG_INF).astype(np.float32)
    head = np.where(band & (ki >= BAND), 0.0, NEG_INF).astype(np.float32)
    both = np.stack([head, full])
    return jnp.asarray(np.concatenate([both, both], axis=1))


def _phase_plan(seq, dilations, shares):
    runs = [(d, r, lo, min(lo + RUN_BLOCKS, seq // d // BAND))
            for d in sorted(dilations, reverse=True) for r in range(d)
            for lo in range(0, seq // d // BAND, RUN_BLOCKS)]
    cost = lambda run: BLOCK_COST[run[0]] * (run[3] - run[2])
    bounds = np.cumsum(shares) * sum(cost(run) for run in runs)
    phases, cur, spent = [], [], 0.0
    for run in runs:
        cur.append(run)
        spent += cost(run)
        if len(phases) < len(shares) - 1 and spent >= bounds[len(phases)]:
            phases.append(tuple(cur))
            cur = []
    phases.append(tuple(cur))
    assert len(phases) == len(shares) and all(phases)
    return tuple(phases)


def _attention(hq3, hb3, sink, hq_s, cache_a_t, cache_b_t, sink_col):
    b, seq, _ = hq3.shape
    n_pairs = A_W // LANES
    phases = tuple(tuple(("a",) + run for run in phase)
                   for phase in _phase_plan(seq, DILATIONS, PHASE_SHARE))
    phases += tuple(tuple(("b",) + run for run in phase) for phase in _phase_plan(seq, (1,), (1.0,)))
    n_ph = len(phases)
    hosts = n_ph - 1
    n = hq_s.shape[0]
    assert n == b * n_pairs * hosts, "every sample sequence has a hosting grid step"
    buf_a = cache_a_t.shape[-1]
    buf_b = cache_b_t.shape[-1]
    kb_t, vb_t = KB_T - QB_T, VB_T - QB_T

    def host(i, u, j):
        return jnp.maximum((i * n_pairs + u) * hosts + j - 1, 0)

    tile = lambda first: (lambda i, u, j: (i, 0, first + u))
    in_specs = [
        pl.BlockSpec((None, seq, LANES), tile(QA_T)),
        pl.BlockSpec((None, seq, LANES), tile(KA_T)),
        pl.BlockSpec((None, seq, LANES), tile(VA_T)),
        pl.BlockSpec((2, 2 * BAND, 2 * BAND), lambda i, u, j: (0, 0, 0)),
        pl.BlockSpec((None, seq, LANES), tile(0)),
        pl.BlockSpec((None, seq, LANES), lambda i, u, j: (i, 0, kb_t)),
        pl.BlockSpec((None, seq, LANES), lambda i, u, j: (i, 0, vb_t)),
        pl.BlockSpec((None, 1, LANES), lambda i, u, j: (u, 0, 0)),
        pl.BlockSpec((None, 1, HQ_W), lambda i, u, j: (host(i, u, j), 0, 0)),
        pl.BlockSpec((None, 2, A_W, buf_a), lambda i, u, j: (host(i, u, j), 0, 0, 0)),
        pl.BlockSpec((None, 2, B_KV_W, buf_b), lambda i, u, j: (host(i, u, j), 0, 0, 0)),
        pl.BlockSpec((1, buf_a), lambda i, u, j: (0, 0)),
        pl.BlockSpec((B_HEADS, 1), lambda i, u, j: (0, 0)),
    ]
    out_tile = pl.BlockSpec((None, seq, LANES), lambda i, u, j: (i, 0, u))
    return pl.pallas_call(
        functools.partial(_attn_kernel, phases=phases, host_phases=tuple(range(1, n_ph))),
        grid=(b, n_pairs, n_ph),
        in_specs=in_specs,
        out_specs=[out_tile, out_tile,
                   pl.BlockSpec((None, 1, A_W), lambda i, u, j: (host(i, u, j), 0, 0)),
                   pl.BlockSpec((None, 1, B_W), lambda i, u, j: (host(i, u, j), 0, 0))],
        out_shape=[jax.ShapeDtypeStruct((b, seq, A_W), jnp.bfloat16),
                   jax.ShapeDtypeStruct((b, seq, B_W), jnp.bfloat16),
                   jax.ShapeDtypeStruct((n, 1, A_W), jnp.float32),
                   jax.ShapeDtypeStruct((n, 1, B_W), jnp.float32)],
        scratch_shapes=[pltpu.VMEM((seq, LANES), jnp.float32)] * 3,
        compiler_params=pltpu.CompilerParams(
            dimension_semantics=("arbitrary", "arbitrary", "arbitrary"),
            vmem_limit_bytes=VMEM_LIMIT),
        name="attn",
    )(hq3, hq3, hq3, _band_bias(), hb3, hb3, hb3, sink,
      hq_s.reshape(n, 1, HQ_W), cache_a_t, cache_b_t, _sample_weights(buf_a), sink_col)


def _out_kernel(x_ref, oa_ref, ob_ref, sg_ref, wo_ref, g_ref, b_ref, y_ref, *, alpha):
    sg = sg_ref[...].astype(jnp.float32)
    mix_a = (oa_ref[...].astype(jnp.float32) * sg[:, :A_W]).astype(jnp.bfloat16)
    mix_b = (ob_ref[...].astype(jnp.float32) * sg[:, A_W:]).astype(jnp.bfloat16)
    out = (jnp.dot(mix_a, wo_ref[:A_W, :], preferred_element_type=jnp.float32)
           + jnp.dot(mix_b, wo_ref[A_W:, :], preferred_element_type=jnp.float32))
    z = alpha * x_ref[...] + out
    mu = jnp.mean(z, axis=1, keepdims=True)
    zc = z - mu
    var = jnp.mean(zc * zc, axis=1, keepdims=True)
    y_ref[...] = zc * lax.rsqrt(var + LN_EPS) * g_ref[...] + b_ref[...]


def _output(x2, oa, ob, sg, wo, g, b, alpha, tm):
    n = x2.shape[0]
    row = lambda i: (i, 0)
    fixed = lambda i: (0, 0)
    return pl.pallas_call(
        functools.partial(_out_kernel, alpha=alpha),
        grid=(n // tm,),
        in_specs=[
            pl.BlockSpec((tm, D_MODEL), row),
            pl.BlockSpec((tm, A_W), row),
            pl.BlockSpec((tm, B_W), row),
            pl.BlockSpec((tm, G_W), row),
            pl.BlockSpec((G_W, D_MODEL), fixed),
            pl.BlockSpec((1, D_MODEL), fixed),
            pl.BlockSpec((1, D_MODEL), fixed),
        ],
        out_specs=pl.BlockSpec((tm, D_MODEL), row),
        out_shape=jax.ShapeDtypeStruct((n, D_MODEL), jnp.float32),
        compiler_params=pltpu.CompilerParams(
            dimension_semantics=("arbitrary",), vmem_limit_bytes=VMEM_LIMIT),
        name="out_proj",
    )(x2, oa, ob, sg, wo, g, b)


def _rope_tables(pos):
    inv = ROPE_THETA ** (-np.arange(0, ROT_DIM, 2, dtype=np.float64) / ROT_DIM)
    ang = pos.astype(np.float64)[:, None] * inv[None, :]
    cos = np.cos(ang)
    sin = np.sin(ang)
    n = pos.shape[0]
    half = ROT_DIM // 2
    pad = HEAD_DIM - ROT_DIM
    one = np.ones((n, pad))
    zero = np.zeros((n, pad))
    zh = np.zeros((n, half))
    cos_h = np.concatenate([cos, cos, one], axis=1)
    sinp_h = np.concatenate([zh, sin, zero], axis=1)
    sinm_h = np.concatenate([-sin, zh, zero], axis=1)
    rep = LANES // HEAD_DIM
    return tuple(jnp.asarray(np.tile(t, (1, rep)), jnp.float32) for t in (cos_h, sinp_h, sinm_h))


def _pair_b_heads(a, axis):
    group = B_HEADS // B_KV_HEADS
    shape = a.shape
    split = shape[:axis] + (B_KV_HEADS, group, shape[axis] // B_HEADS) + shape[axis + 1:]
    return jnp.swapaxes(a.reshape(split), axis, axis + 1).reshape(shape)


def _kv_rows(kvt, heads):
    n, _, _, rows = kvt.shape
    return jnp.transpose(kvt.reshape(n, 2, heads, HEAD_DIM, rows), (0, 4, 1, 2, 3))


def _layer(xp, xs, cache_a, cache_b, w_in, sinks, w_o, ln_g, ln_b, alpha, past_len):
    b, seq, _ = xp.shape
    nb = xs.shape[0]
    qa, ka, va, ga, qb, kb, vb, gb = jnp.split(
        w_in, np.cumsum([A_W, A_W, A_W, A_W, B_W, B_KV_W, B_KV_W])[:].tolist(), axis=1)
    q_scale = HEAD_DIM ** -0.5 * LOG2E
    w = jnp.concatenate([qa * q_scale, ka, va, _pair_b_heads(qb, 1) * q_scale, kb, vb,
                         ga, _pair_b_heads(gb, 1)], axis=1).astype(jnp.bfloat16)
    wo = jnp.concatenate([w_o[:A_W], _pair_b_heads(w_o[A_W:], 0)], axis=0).astype(jnp.bfloat16)
    sink_p = _pair_b_heads(sinks.astype(jnp.float32) * LOG2E, 0)
    sink_tiles = jnp.repeat(sink_p, HEAD_DIM).reshape(B_W // LANES, 1, LANES)
    g2 = ln_g.reshape(1, D_MODEL)
    b2 = ln_b.reshape(1, D_MODEL)

    tm = 512
    wa = min(A_WINDOW, seq)
    wb = min(B_WINDOW, seq)
    x2 = xp.reshape(b * seq, D_MODEL)
    hq, sg, kva_t, kvb_t, hb = _project(x2, w, *_rope_tables(np.arange(seq)), tm, seq, wa, wb)
    hq3 = hq.reshape(b, seq, HQ_W)

    xs2 = xs.reshape(nb, D_MODEL)
    pos_s = np.full((nb,), past_len, np.int64)
    hq_s, sg_s, new_a_t, new_b_t, _ = _project(xs2, w, *_rope_tables(pos_s), nb, nb, nb, nb)
    ca_t = jnp.transpose(cache_a, (0, 2, 3, 4, 1)).reshape(nb, 2, A_W, cache_a.shape[1])
    cb_t = jnp.transpose(cache_b, (0, 2, 3, 4, 1)).reshape(nb, 2, B_KV_W, cache_b.shape[1])

    oa, ob, oa_s, ob_s = _attention(hq3, hb.reshape(b, seq, HB_W), sink_tiles,
                                    hq_s, ca_t, cb_t, sink_p.reshape(B_HEADS, 1))
    y = _output(x2, oa.reshape(b * seq, A_W), ob.reshape(b * seq, B_W), sg, wo, g2, b2, alpha, 2 * tm)
    kv_a = _kv_rows(kva_t, A_HEADS)
    kv_b = _kv_rows(kvb_t, B_KV_HEADS)

    y_s = _output(xs2, oa_s.reshape(nb, A_W), ob_s.reshape(nb, B_W), sg_s, wo, g2, b2, alpha, nb)
    new_a = _kv_rows(new_a_t, A_HEADS).reshape(nb, 1, 2, A_HEADS, HEAD_DIM)
    new_b = _kv_rows(new_b_t, B_KV_HEADS).reshape(nb, 1, 2, B_KV_HEADS, HEAD_DIM)
    return (y.reshape(b, seq, D_MODEL), y_s.reshape(nb, 1, D_MODEL), kv_a, kv_b, new_a, new_b)


def kernel(x_prompt, x_sample, cache_a_kv, cache_b_kv, w_in, attn_sinks, w_o, ln_g, ln_b):
    depth = w_in.shape[0]
    assert depth == 1 and x_sample.shape[1] == 1, "single layer, one sample token per sequence"
    assert cache_a_kv.shape[2] == BAND * max(DILATIONS), "mixer-A cache covers every strided read"
    assert cache_b_kv.shape[2] == B_WINDOW
    alpha = (2 * depth) ** 0.25
    outs = _layer(x_prompt, x_sample, cache_a_kv[0], cache_b_kv[0], w_in[0], attn_sinks[0],
                  w_o[0], ln_g[0], ln_b[0], alpha, PAST_LEN)
    yp, ys, kv_a, kv_b, new_a, new_b = outs
    return (yp, ys, kv_a[None], kv_b[None], new_a[None], new_b[None])
```

```python
import functools

import jax
import jax.numpy as jnp
import numpy as np
from jax import lax
from jax.experimental import pallas as pl
from jax.experimental.pallas import tpu as pltpu

D_MODEL = 1024
HEAD_DIM = 64
A_HEADS = 8
B_HEADS = 8
B_KV_HEADS = 2
DILATIONS = (1, 4, 16)
BAND = 128
A_WINDOW = 2048
B_WINDOW = 128
PAST_LEN = 16384
ROT_DIM = HEAD_DIM // 4
ROPE_THETA = 500000.0
LN_EPS = 1e-5
A_W = A_HEADS * HEAD_DIM
B_W = B_HEADS * HEAD_DIM
B_KV_W = B_KV_HEADS * HEAD_DIM
LANES = 128
HQ_W = 3 * A_W + B_W + 2 * B_KV_W
G_W = A_W + B_W
IN_WIDTH = HQ_W + G_W
QA_T, KA_T, VA_T = 0, A_W // LANES, 2 * A_W // LANES
QB_T = 3 * A_W // LANES
KB_T = QB_T + B_W // LANES
VB_T = KB_T + 1
VMEM_LIMIT = 56 * 1024 * 1024
COL_CHUNK = 256
NEG_INF = float("-inf")
LOG2E = 1.4426950408889634
BLOCK_COST = {1: 1.0, 4: 1.2, 16: 1.55}
PHASE_SHARE = (0.29, 0.236, 0.237, 0.237)
RUN_BLOCKS = 8


def _proj_kernel(x_ref, w_ref, cos_ref, sinp_ref, sinm_ref, hq_ref, sg_ref, kva_ref, kvb_ref, *, wb):
    tm = x_ref.shape[0]
    x = x_ref[...].astype(jnp.bfloat16)
    cos = cos_ref[...]
    sinp = sinp_ref[...]
    sinm = sinm_ref[...]
    rope_tiles = set(range(QA_T, VA_T)) | set(range(QB_T, VB_T))
    for c in range(IN_WIDTH // COL_CHUNK):
        acc = jnp.dot(x, w_ref[:, c * COL_CHUNK:(c + 1) * COL_CHUNK],
                      preferred_element_type=jnp.float32)
        for half in range(COL_CHUNK // LANES):
            tile = c * (COL_CHUNK // LANES) + half
            t = acc[:, half * LANES:(half + 1) * LANES]
            if tile < HQ_W // LANES:
                if tile in rope_tiles:
                    t = (t * cos + pltpu.roll(t, ROT_DIM // 2, 1) * sinp
                         + pltpu.roll(t, LANES - ROT_DIM // 2, 1) * sinm)
                hq_ref[:, tile * LANES:(tile + 1) * LANES] = t.astype(hq_ref.dtype)
                if KA_T <= tile < QB_T:
                    kv, j = divmod(tile - KA_T, A_W // LANES)
                    kva_ref[kv, j * LANES:(j + 1) * LANES, :] = t.T
                elif tile >= KB_T:
                    kvb_ref[tile - KB_T] = t[tm - wb:].T
            else:
                g = tile - HQ_W // LANES
                sg_ref[:, g * LANES:(g + 1) * LANES] = (t * jax.nn.sigmoid(t)).astype(sg_ref.dtype)


def _project(x2, w, cos, sinp, sinm, tm, seq, wa, wb, hq_dtype):
    n = x2.shape[0]
    tab_blocks = cos.shape[0] // tm
    tps = seq // tm
    first_a = tps - wa // tm
    assert seq % tm == 0 and wa % tm == 0 and wb <= tm and cos.shape[0] == seq
    row = lambda i: (i, 0)
    tab = lambda i: (i % tab_blocks, 0)
    return pl.pallas_call(
        functools.partial(_proj_kernel, wb=wb),
        grid=(n // tm,),
        in_specs=[
            pl.BlockSpec((tm, D_MODEL), row),
            pl.BlockSpec((D_MODEL, IN_WIDTH), lambda i: (0, 0)),
            pl.BlockSpec((tm, LANES), tab),
            pl.BlockSpec((tm, LANES), tab),
            pl.BlockSpec((tm, LANES), tab),
        ],
        out_specs=[
            pl.BlockSpec((tm, HQ_W), row),
            pl.BlockSpec((tm, G_W), row),
            pl.BlockSpec((None, 2, A_W, tm),
                         lambda i: (i // tps, 0, 0, jnp.maximum(i % tps - first_a, 0))),
            pl.BlockSpec((None, 2, B_KV_W, wb), lambda i: (i // tps, 0, 0, 0)),
        ],
        out_shape=[jax.ShapeDtypeStruct((n, HQ_W), hq_dtype),
                   jax.ShapeDtypeStruct((n, G_W), jnp.bfloat16),
                   jax.ShapeDtypeStruct((n // seq, 2, A_W, wa), jnp.float32),
                   jax.ShapeDtypeStruct((n // seq, 2, B_KV_W, wb), jnp.float32)],
        compiler_params=pltpu.CompilerParams(
            dimension_semantics=("arbitrary",), vmem_limit_bytes=VMEM_LIMIT),
        name="proj",
    )(x2, w, cos, sinp, sinm)


def _lane_lo(shape):
    return lax.broadcasted_iota(jnp.int32, shape, len(shape) - 1) < HEAD_DIM


def _sample_step(hq_ref, ca_ref, cb_ref, w_ref, sink_ref, oa_ref, ob_ref):
    row = hq_ref[...]
    qa = row[:, QA_T * LANES:KA_T * LANES]
    ka_new = row[:, KA_T * LANES:VA_T * LANES]
    va_new = row[:, VA_T * LANES:QB_T * LANES]
    qb = row[:, QB_T * LANES:KB_T * LANES]
    kb_new = row[:, KB_T * LANES:VB_T * LANES]
    vb_new = row[:, VB_T * LANES:HQ_W]
    buf_a = ca_ref.shape[-1]

    hrow = lax.broadcasted_iota(jnp.int32, (A_HEADS, A_W), 0)
    hcol = lax.broadcasted_iota(jnp.int32, (A_HEADS, A_W), 1) // HEAD_DIM
    own = hrow == hcol

    def spread(col):
        return jnp.sum(jnp.where(own, jnp.broadcast_to(col, (A_HEADS, A_W)), 0.0), axis=0, keepdims=True)

    q_bd = jnp.where(own, jnp.broadcast_to(qa, (A_HEADS, A_W)), 0.0)
    s_new = jnp.sum(q_bd * ka_new, axis=1, keepdims=True)
    w = w_ref[...]
    s = jnp.dot(q_bd.astype(jnp.bfloat16), ca_ref[0].astype(jnp.bfloat16),
                preferred_element_type=jnp.float32)
    s = jnp.where(w > 0.0, s, NEG_INF)
    m = jnp.maximum(jnp.max(s, axis=1, keepdims=True), s_new)
    p = jnp.exp2(s - m) * w
    p_new = float(len(DILATIONS)) * jnp.exp2(s_new - m)
    den = jnp.sum(p, axis=1, keepdims=True) + p_new
    folded = []
    for h in range(A_HEADS):
        acc = None
        for t in range(buf_a // LANES):
            cols = slice(t * LANES, (t + 1) * LANES)
            part = ca_ref[1, h * HEAD_DIM:(h + 1) * HEAD_DIM, cols] * p[h:h + 1, cols]
            acc = part if acc is None else acc + part
        folded.append(acc)
    o = jnp.sum(jnp.concatenate(folded, axis=0).T, axis=0, keepdims=True)
    oa_ref[...] = (o + spread(p_new) * va_new) / spread(den)

    prow = lax.broadcasted_iota(jnp.int32, (B_HEADS, LANES), 0)
    pcol = lax.broadcasted_iota(jnp.int32, (B_HEADS, LANES), 1) // HEAD_DIM
    qb_rows = jnp.zeros((B_HEADS, LANES), jnp.float32)
    for t in range(B_W // LANES):
        qb_rows = jnp.where(prow // 2 == t,
                            jnp.broadcast_to(qb[:, t * LANES:(t + 1) * LANES], (B_HEADS, LANES)), qb_rows)
    own_b = (prow % 2) == pcol
    qb_bd = jnp.where(own_b, qb_rows, 0.0)
    sb = jnp.dot(qb_bd.astype(jnp.bfloat16), cb_ref[0].astype(jnp.bfloat16),
                 preferred_element_type=jnp.float32)
    sb_new = jnp.sum(qb_bd * kb_new, axis=1, keepdims=True)
    sink = sink_ref[...]
    mb = jnp.maximum(jnp.maximum(jnp.max(sb, axis=1, keepdims=True), sb_new), sink)
    pb = jnp.exp2(sb - mb)
    pb_new = jnp.exp2(sb_new - mb)
    den_b = jnp.sum(pb, axis=1, keepdims=True) + pb_new + jnp.exp2(sink - mb)
    ob = lax.dot_general(pb.astype(jnp.bfloat16), cb_ref[1].astype(jnp.bfloat16),
                         (((1,), (1,)), ((), ())), preferred_element_type=jnp.float32)
    ob = (ob + pb_new * vb_new) / den_b
    lo = _lane_lo((1, LANES))
    for t in range(B_W // LANES):
        ob_ref[:, t * LANES:(t + 1) * LANES] = jnp.where(lo, ob[2 * t:2 * t + 1], ob[2 * t + 1:2 * t + 2])


def _sample_weights(buf):
    dist = buf - np.arange(buf)
    w = np.zeros((buf,), np.float32)
    for d in DILATIONS:
        w += ((dist % d == 0) & (dist <= BAND * d)).astype(np.float32)
    return jnp.asarray(w[None, :])


def _attn_kernel(qa_ref, ka_ref, va_ref, bias_ref, qb_ref, kb_ref, vb_ref, sink_ref,
                 hq_s_ref, ca_ref, cb_ref, w_ref, sink_col_ref,
                 oa_ref, ob_ref, oa_s_ref, ob_s_ref, acc_ref, m_ref, l_ref, qf_ref, kf_ref, vf_ref,
                 *, phases, host_phases):
    sample_refs = (hq_s_ref, ca_ref, cb_ref, w_ref, sink_col_ref, oa_s_ref, ob_s_ref)
    dils = [d for phase in phases for mixer, d, _, _, _ in phase if mixer == "a"]
    lo = _lane_lo((BAND, LANES))

    def pair(a, b):
        return jnp.where(lo, a, b)

    head_lanes = (jnp.where(lo, 1.0, 0.0).astype(jnp.bfloat16),
                  jnp.where(lo, 0.0, 1.0).astype(jnp.bfloat16))

    def by_head(a):
        return a * head_lanes[0], a * head_lanes[1]

    ones = tuple(jnp.concatenate([t, t], axis=0) for t in head_lanes)

    def run(mixer, d, r, i_lo, i_hi):
        if mixer == "a":
            q_ref, k_ref, v_ref = (qa_ref, ka_ref, va_ref) if d == 1 else (qf_ref, kf_ref, vf_ref)
            o_ref = oa_ref
            multi, with_sink = len(set(dils)) > 1, False
            first, last = d == dils[0], d == dils[-1]
        else:
            q_ref, k_ref, v_ref, o_ref = qb_ref, kb_ref, vb_ref, ob_ref
            multi, with_sink = False, True
            first = last = True

        def rows(start):
            return pl.ds(start, BAND) if d == 1 else pl.ds(start, BAND, stride=d)

        def block(i, carry):
            kp, vp = carry
            start = r + i * (BAND * d)
            q = q_ref[rows(start), :]
            kc = k_ref[rows(start), :].astype(jnp.bfloat16)
            vc = by_head(v_ref[rows(start), :].astype(jnp.bfloat16))
            if q.dtype == jnp.bfloat16:
                q2 = jnp.concatenate(by_head(q), axis=0)
            else:
                q2 = jnp.concatenate([jnp.where(lo, q, 0.0), jnp.where(lo, 0.0, q)],
                                     axis=0).astype(jnp.bfloat16)
            kk = jnp.concatenate([kp, kc], axis=0)
            s = lax.dot_general(q2, kk, (((1,), (1,)), ((), ())),
                                preferred_element_type=jnp.float32)
            s = s + bias_ref[min(i, 1)]
            m = jnp.max(s, axis=1, keepdims=True)
            p = jnp.exp2(s - m).astype(jnp.bfloat16)
            vv = jnp.concatenate(
                [jnp.concatenate([jnp.concatenate([vp[h], vc[h]], axis=0), ones[h]], axis=1)
                 for h in range(2)], axis=0)
            pv = jnp.dot(jnp.concatenate([p[:BAND], p[BAND:]], axis=1), vv,
                         preferred_element_type=jnp.float32)
            o_n = pv[:, :LANES]
            l_n = pv[:, LANES:]
            m_n = pair(jnp.broadcast_to(m[:BAND], (BAND, LANES)),
                       jnp.broadcast_to(m[BAND:], (BAND, LANES)))
            if multi and not first:
                m_o = m_ref[rows(start), :]
                m_t = jnp.maximum(m_o, m_n)
                a_o = jnp.exp2(m_o - m_t)
                a_n = jnp.exp2(m_n - m_t)
                o_n = acc_ref[rows(start), :] * a_o + o_n * a_n
                l_n = l_ref[rows(start), :] * a_o + l_n * a_n
                m_n = m_t
            if last:
                if with_sink:
                    sink = sink_ref[...]
                    m_t = jnp.maximum(m_n, sink)
                    a_n = jnp.exp2(m_n - m_t)
                    o_n = o_n * a_n
                    l_n = l_n * a_n + jnp.exp2(sink - m_t)
                o_ref[rows(start), :] = (o_n / l_n).astype(o_ref.dtype)
            else:
                acc_ref[rows(start), :] = o_n
                l_ref[rows(start), :] = l_n
                m_ref[rows(start), :] = m_n
            return kc, vc

        if i_lo == 0:
            zero = jnp.zeros((BAND, LANES), jnp.bfloat16)
            carry = (zero, (zero, zero))
        else:
            start = r + (i_lo - 1) * (BAND * d)
            carry = (k_ref[rows(start), :].astype(jnp.bfloat16),
                     by_head(v_ref[rows(start), :].astype(jnp.bfloat16)))
        for i in range(i_lo, i_hi):
            carry = block(i, carry)

    def run_phase(ph):
        if ph == 0:
            qf_ref[...] = qa_ref[...].astype(jnp.float32)
            kf_ref[...] = ka_ref[...].astype(jnp.float32)
            vf_ref[...] = va_ref[...].astype(jnp.float32)
        if ph in host_phases:
            _sample_step(*sample_refs)
        for seg in phases[ph]:
            run(*seg)

    for ph in range(len(phases)):
        pl.when(pl.program_id(2) == ph)(functools.partial(run_phase, ph))


def _band_bias():
    qi = np.arange(BAND)[:, None]
    ki = np.arange(2 * BAND)[None, :]
    dist = BAND + qi - ki
    band = (dist >= 0) & (dist <= BAND)
    full = np.where(band, 0.0, NEG_INF).astype(np.float32)
    head = np.where(band & (ki >= BAND), 0.0, NEG_INF).astype(np.float32)
    both = np.stack([head, full])
    return jnp.asarray(np.concatenate([both, both], axis=1))


def _phase_plan(seq, dilations, shares):
    runs = [(d, r, lo, min(lo + RUN_BLOCKS, seq // d // BAND))
            for d in sorted(dilations, reverse=True) for r in range(d)
            for lo in range(0, seq // d // BAND, RUN_BLOCKS)]
    cost = lambda run: BLOCK_COST[run[0]] * (run[3] - run[2])
    bounds = np.cumsum(shares) * sum(cost(run) for run in runs)
    phases, cur, spent = [], [], 0.0
    for run in runs:
        cur.append(run)
        spent += cost(run)
        if len(phases) < len(shares) - 1 and spent >= bounds[len(phases)]:
            phases.append(tuple(cur))
            cur = []
    phases.append(tuple(cur))
    assert len(phases) == len(shares) and all(phases)
    return tuple(phases)


def _attention(hq3, sink, hq_s, cache_a_t, cache_b_t, sink_col):
    b, seq, _ = hq3.shape
    n_pairs = A_W // LANES
    phases = tuple(tuple(("a",) + run for run in phase)
                   for phase in _phase_plan(seq, DILATIONS, PHASE_SHARE))
    phases += tuple(tuple(("b",) + run for run in phase) for phase in _phase_plan(seq, (1,), (1.0,)))
    n_ph = len(phases)
    hosts = n_ph - 1
    n = hq_s.shape[0]
    assert n == b * n_pairs * hosts, "every sample sequence has a hosting grid step"
    buf_a = cache_a_t.shape[-1]
    buf_b = cache_b_t.shape[-1]

    def host(i, u, j):
        return jnp.maximum((i * n_pairs + u) * hosts + j - 1, 0)

    tile = lambda first: (lambda i, u, j: (i, 0, first + u))
    in_specs = [
        pl.BlockSpec((None, seq, LANES), tile(QA_T)),
        pl.BlockSpec((None, seq, LANES), tile(KA_T)),
        pl.BlockSpec((None, seq, LANES), tile(VA_T)),
        pl.BlockSpec((2, 2 * BAND, 2 * BAND), lambda i, u, j: (0, 0, 0)),
        pl.BlockSpec((None, seq, LANES), tile(QB_T)),
        pl.BlockSpec((None, seq, LANES), lambda i, u, j: (i, 0, KB_T)),
        pl.BlockSpec((None, seq, LANES), lambda i, u, j: (i, 0, VB_T)),
        pl.BlockSpec((None, 1, LANES), lambda i, u, j: (u, 0, 0)),
        pl.BlockSpec((None, 1, HQ_W), lambda i, u, j: (host(i, u, j), 0, 0)),
        pl.BlockSpec((None, 2, A_W, buf_a), lambda i, u, j: (host(i, u, j), 0, 0, 0)),
        pl.BlockSpec((None, 2, B_KV_W, buf_b), lambda i, u, j: (host(i, u, j), 0, 0, 0)),
        pl.BlockSpec((1, buf_a), lambda i, u, j: (0, 0)),
        pl.BlockSpec((B_HEADS, 1), lambda i, u, j: (0, 0)),
    ]
    out_tile = pl.BlockSpec((None, seq, LANES), lambda i, u, j: (i, 0, u))
    return pl.pallas_call(
        functools.partial(_attn_kernel, phases=phases, host_phases=tuple(range(1, n_ph))),
        grid=(b, n_pairs, n_ph),
        in_specs=in_specs,
        out_specs=[out_tile, out_tile,
                   pl.BlockSpec((None, 1, A_W), lambda i, u, j: (host(i, u, j), 0, 0)),
                   pl.BlockSpec((None, 1, B_W), lambda i, u, j: (host(i, u, j), 0, 0))],
        out_shape=[jax.ShapeDtypeStruct((b, seq, A_W), jnp.bfloat16),
                   jax.ShapeDtypeStruct((b, seq, B_W), jnp.bfloat16),
                   jax.ShapeDtypeStruct((n, 1, A_W), jnp.float32),
                   jax.ShapeDtypeStruct((n, 1, B_W), jnp.float32)],
        scratch_shapes=[pltpu.VMEM((seq, LANES), jnp.float32)] * 6,
        compiler_params=pltpu.CompilerParams(
            dimension_semantics=("arbitrary", "arbitrary", "arbitrary"),
            vmem_limit_bytes=VMEM_LIMIT),
        name="attn",
    )(hq3, hq3, hq3, _band_bias(), hq3, hq3, hq3, sink,
      hq_s.reshape(n, 1, HQ_W), cache_a_t, cache_b_t, _sample_weights(buf_a), sink_col)


def _out_kernel(x_ref, oa_ref, ob_ref, sg_ref, wo_ref, g_ref, b_ref, y_ref, *, alpha):
    sg = sg_ref[...].astype(jnp.float32)
    mix_a = (oa_ref[...].astype(jnp.float32) * sg[:, :A_W]).astype(jnp.bfloat16)
    mix_b = (ob_ref[...].astype(jnp.float32) * sg[:, A_W:]).astype(jnp.bfloat16)
    out = (jnp.dot(mix_a, wo_ref[:A_W, :], preferred_element_type=jnp.float32)
           + jnp.dot(mix_b, wo_ref[A_W:, :], preferred_element_type=jnp.float32))
    z = alpha * x_ref[...] + out
    mu = jnp.mean(z, axis=1, keepdims=True)
    zc = z - mu
    var = jnp.mean(zc * zc, axis=1, keepdims=True)
    y_ref[...] = zc * lax.rsqrt(var + LN_EPS) * g_ref[...] + b_ref[...]


def _output(x2, oa, ob, sg, wo, g, b, alpha, tm):
    n = x2.shape[0]
    row = lambda i: (i, 0)
    fixed = lambda i: (0, 0)
    return pl.pallas_call(
        functools.partial(_out_kernel, alpha=alpha),
        grid=(n // tm,),
        in_specs=[
            pl.BlockSpec((tm, D_MODEL), row),
            pl.BlockSpec((tm, A_W), row),
            pl.BlockSpec((tm, B_W), row),
            pl.BlockSpec((tm, G_W), row),
            pl.BlockSpec((G_W, D_MODEL), fixed),
            pl.BlockSpec((1, D_MODEL), fixed),
            pl.BlockSpec((1, D_MODEL), fixed),
        ],
        out_specs=pl.BlockSpec((tm, D_MODEL), row),
        out_shape=jax.ShapeDtypeStruct((n, D_MODEL), jnp.float32),
        compiler_params=pltpu.CompilerParams(
            dimension_semantics=("arbitrary",), vmem_limit_bytes=VMEM_LIMIT),
        name="out_proj",
    )(x2, oa, ob, sg, wo, g, b)


def _rope_tables(pos):
    inv = ROPE_THETA ** (-np.arange(0, ROT_DIM, 2, dtype=np.float64) / ROT_DIM)
    ang = pos.astype(np.float64)[:, None] * inv[None, :]
    cos = np.cos(ang)
    sin = np.sin(ang)
    n = pos.shape[0]
    half = ROT_DIM // 2
    pad = HEAD_DIM - ROT_DIM
    one = np.ones((n, pad))
    zero = np.zeros((n, pad))
    zh = np.zeros((n, half))
    cos_h = np.concatenate([cos, cos, one], axis=1)
    sinp_h = np.concatenate([zh, sin, zero], axis=1)
    sinm_h = np.concatenate([-sin, zh, zero], axis=1)
    rep = LANES // HEAD_DIM
    return tuple(jnp.asarray(np.tile(t, (1, rep)), jnp.float32) for t in (cos_h, sinp_h, sinm_h))


def _pair_b_heads(a, axis):
    group = B_HEADS // B_KV_HEADS
    shape = a.shape
    split = shape[:axis] + (B_KV_HEADS, group, shape[axis] // B_HEADS) + shape[axis + 1:]
    return jnp.swapaxes(a.reshape(split), axis, axis + 1).reshape(shape)


def _kv_rows(kvt, heads):
    n, _, _, rows = kvt.shape
    return jnp.transpose(kvt.reshape(n, 2, heads, HEAD_DIM, rows), (0, 4, 1, 2, 3))


def _layer(xp, xs, cache_a, cache_b, w_in, sinks, w_o, ln_g, ln_b, alpha, past_len):
    b, seq, _ = xp.shape
    nb = xs.shape[0]
    qa, ka, va, ga, qb, kb, vb, gb = jnp.split(
        w_in, np.cumsum([A_W, A_W, A_W, A_W, B_W, B_KV_W, B_KV_W])[:].tolist(), axis=1)
    q_scale = HEAD_DIM ** -0.5 * LOG2E
    w = jnp.concatenate([qa * q_scale, ka, va, _pair_b_heads(qb, 1) * q_scale, kb, vb,
                         ga, _pair_b_heads(gb, 1)], axis=1).astype(jnp.bfloat16)
    wo = jnp.concatenate([w_o[:A_W], _pair_b_heads(w_o[A_W:], 0)], axis=0).astype(jnp.bfloat16)
    sink_p = _pair_b_heads(sinks.astype(jnp.float32) * LOG2E, 0)
    sink_tiles = jnp.repeat(sink_p, HEAD_DIM).reshape(B_W // LANES, 1, LANES)
    g2 = ln_g.reshape(1, D_MODEL)
    b2 = ln_b.reshape(1, D_MODEL)

    tm = 512
    wa = min(A_WINDOW, seq)
    wb = min(B_WINDOW, seq)
    x2 = xp.reshape(b * seq, D_MODEL)
    hq, sg, kva_t, kvb_t = _project(x2, w, *_rope_tables(np.arange(seq)), tm, seq, wa, wb, jnp.bfloat16)
    hq3 = hq.reshape(b, seq, HQ_W)

    xs2 = xs.reshape(nb, D_MODEL)
    pos_s = np.full((nb,), past_len, np.int64)
    hq_s, sg_s, new_a_t, new_b_t = _project(xs2, w, *_rope_tables(pos_s), nb, nb, nb, nb, jnp.float32)
    ca_t = jnp.transpose(cache_a, (0, 2, 3, 4, 1)).reshape(nb, 2, A_W, cache_a.shape[1])
    cb_t = jnp.transpose(cache_b, (0, 2, 3, 4, 1)).reshape(nb, 2, B_KV_W, cache_b.shape[1])

    oa, ob, oa_s, ob_s = _attention(hq3, sink_tiles,
                                    hq_s, ca_t, cb_t, sink_p.reshape(B_HEADS, 1))
    y = _output(x2, oa.reshape(b * seq, A_W), ob.reshape(b * seq, B_W), sg, wo, g2, b2, alpha, 2 * tm)
    kv_a = _kv_rows(kva_t, A_HEADS)
    kv_b = _kv_rows(kvb_t, B_KV_HEADS)

    y_s = _output(xs2, oa_s.reshape(nb, A_W), ob_s.reshape(nb, B_W), sg_s, wo, g2, b2, alpha, nb)
    new_a = _kv_rows(new_a_t, A_HEADS).reshape(nb, 1, 2, A_HEADS, HEAD_DIM)
    new_b = _kv_rows(new_b_t, B_KV_HEADS).reshape(nb, 1, 2, B_KV_HEADS, HEAD_DIM)
    return (y.reshape(b, seq, D_MODEL), y_s.reshape(nb, 1, D_MODEL), kv_a, kv_b, new_a, new_b)


def kernel(x_prompt, x_sample, cache_a_kv, cache_b_kv, w_in, attn_sinks, w_o, ln_g, ln_b):
    depth = w_in.shape[0]
    assert depth == 1 and x_sample.shape[1] == 1, "single layer, one sample token per sequence"
    assert cache_a_kv.shape[2] == BAND * max(DILATIONS), "mixer-A cache covers every strided read"
    assert cache_b_kv.shape[2] == B_WINDOW
    alpha = (2 * depth) ** 0.25
    outs = _layer(x_prompt, x_sample, cache_a_kv[0], cache_b_kv[0], w_in[0], attn_sinks[0],
                  w_o[0], ln_g[0], ln_b[0], alpha, PAST_LEN)
    yp, ys, kv_a, kv_b, new_a, new_b = outs
    return (yp, ys, kv_a[None], kv_b[None], new_a[None], new_b[None])
```

```python
import functools

import jax
import jax.numpy as jnp
import numpy as np
from jax import lax
from jax.experimental import pallas as pl
from jax.experimental.pallas import tpu as pltpu

D_MODEL = 1024
HEAD_DIM = 64
A_HEADS = 8
B_HEADS = 8
B_KV_HEADS = 2
DILATIONS = (1, 4, 16)
BAND = 128
A_WINDOW = 2048
B_WINDOW = 128
PAST_LEN = 16384
ROT_DIM = HEAD_DIM // 4
ROPE_THETA = 500000.0
LN_EPS = 1e-5
A_W = A_HEADS * HEAD_DIM
B_W = B_HEADS * HEAD_DIM
B_KV_W = B_KV_HEADS * HEAD_DIM
LANES = 128
HQ_W = 3 * A_W + B_W + 2 * B_KV_W
HB_W = B_W + 2 * B_KV_W
G_W = A_W + B_W
IN_WIDTH = HQ_W + G_W
QA_T, KA_T, VA_T = 0, A_W // LANES, 2 * A_W // LANES
QB_T = 3 * A_W // LANES
KB_T = QB_T + B_W // LANES
VB_T = KB_T + 1
VMEM_LIMIT = 56 * 1024 * 1024
COL_CHUNK = 256
NEG_INF = float("-inf")
LOG2E = 1.4426950408889634
BLOCK_COST = {1: 1.0, 4: 1.2, 16: 1.55}
PHASE_SHARE = (0.29, 0.236, 0.237, 0.237)
RUN_BLOCKS = 8


def _proj_kernel(x_ref, w_ref, cos_ref, sinp_ref, sinm_ref, hq_ref, sg_ref, kva_ref, kvb_ref, hb_ref,
                 *, wb):
    tm = x_ref.shape[0]
    cos = cos_ref[...]
    sinp = sinp_ref[...]
    sinm = sinm_ref[...]
    rope_tiles = set(range(QA_T, VA_T)) | set(range(QB_T, VB_T))
    for c in range(IN_WIDTH // COL_CHUNK):
        acc = jnp.dot(x_ref[...].astype(jnp.bfloat16), w_ref[:, c * COL_CHUNK:(c + 1) * COL_CHUNK],
                      preferred_element_type=jnp.float32)
        for half in range(COL_CHUNK // LANES):
            tile = c * (COL_CHUNK // LANES) + half
            t = acc[:, half * LANES:(half + 1) * LANES]
            if tile < HQ_W // LANES:
                if tile in rope_tiles:
                    t = (t * cos + pltpu.roll(t, ROT_DIM // 2, 1) * sinp
                         + pltpu.roll(t, LANES - ROT_DIM // 2, 1) * sinm)
                hq_ref[:, tile * LANES:(tile + 1) * LANES] = t
                if KA_T <= tile < QB_T:
                    kv, j = divmod(tile - KA_T, A_W // LANES)
                    kva_ref[kv, j * LANES:(j + 1) * LANES, :] = t.T
                elif tile >= KB_T:
                    kvb_ref[tile - KB_T] = t[tm - wb:].T
                if tile >= QB_T:
                    hb_ref[:, (tile - QB_T) * LANES:(tile - QB_T + 1) * LANES] = t.astype(hb_ref.dtype)
            else:
                g = tile - HQ_W // LANES
                sg_ref[:, g * LANES:(g + 1) * LANES] = (t * jax.nn.sigmoid(t)).astype(sg_ref.dtype)


def _project(x2, w, cos, sinp, sinm, tm, seq, wa, wb):
    n = x2.shape[0]
    tab_blocks = cos.shape[0] // tm
    tps = seq // tm
    first_a = tps - wa // tm
    assert seq % tm == 0 and wa % tm == 0 and wb <= tm and cos.shape[0] == seq
    row = lambda i: (i, 0)
    tab = lambda i: (i % tab_blocks, 0)
    return pl.pallas_call(
        functools.partial(_proj_kernel, wb=wb),
        grid=(n // tm,),
        in_specs=[
            pl.BlockSpec((tm, D_MODEL), row),
            pl.BlockSpec((D_MODEL, IN_WIDTH), lambda i: (0, 0)),
            pl.BlockSpec((tm, LANES), tab),
            pl.BlockSpec((tm, LANES), tab),
            pl.BlockSpec((tm, LANES), tab),
        ],
        out_specs=[
            pl.BlockSpec((tm, HQ_W), row),
            pl.BlockSpec((tm, G_W), row),
            pl.BlockSpec((None, 2, A_W, tm),
                         lambda i: (i // tps, 0, 0, jnp.maximum(i % tps - first_a, 0))),
            pl.BlockSpec((None, 2, B_KV_W, wb), lambda i: (i // tps, 0, 0, 0)),
            pl.BlockSpec((tm, HB_W), row),
        ],
        out_shape=[jax.ShapeDtypeStruct((n, HQ_W), jnp.float32),
                   jax.ShapeDtypeStruct((n, G_W), jnp.bfloat16),
                   jax.ShapeDtypeStruct((n // seq, 2, A_W, wa), jnp.float32),
                   jax.ShapeDtypeStruct((n // seq, 2, B_KV_W, wb), jnp.float32),
                   jax.ShapeDtypeStruct((n, HB_W), jnp.bfloat16)],
        compiler_params=pltpu.CompilerParams(
            dimension_semantics=("arbitrary",), vmem_limit_bytes=VMEM_LIMIT),
        name="proj",
    )(x2, w, cos, sinp, sinm)


def _lane_lo(shape):
    return lax.broadcasted_iota(jnp.int32, shape, len(shape) - 1) < HEAD_DIM


def _sample_step(hq_ref, ca_ref, cb_ref, w_ref, sink_ref, oa_ref, ob_ref):
    row = hq_ref[...]
    qa = row[:, QA_T * LANES:KA_T * LANES]
    ka_new = row[:, KA_T * LANES:VA_T * LANES]
    va_new = row[:, VA_T * LANES:QB_T * LANES]
    qb = row[:, QB_T * LANES:KB_T * LANES]
    kb_new = row[:, KB_T * LANES:VB_T * LANES]
    vb_new = row[:, VB_T * LANES:HQ_W]
    buf_a = ca_ref.shape[-1]

    hrow = lax.broadcasted_iota(jnp.int32, (A_HEADS, A_W), 0)
    hcol = lax.broadcasted_iota(jnp.int32, (A_HEADS, A_W), 1) // HEAD_DIM
    own = hrow == hcol

    def spread(col):
        return jnp.sum(jnp.where(own, jnp.broadcast_to(col, (A_HEADS, A_W)), 0.0), axis=0, keepdims=True)

    q_bd = jnp.where(own, jnp.broadcast_to(qa, (A_HEADS, A_W)), 0.0)
    s_new = jnp.sum(q_bd * ka_new, axis=1, keepdims=True)
    w = w_ref[...]
    s = jnp.dot(q_bd.astype(jnp.bfloat16), ca_ref[0].astype(jnp.bfloat16),
                preferred_element_type=jnp.float32)
    s = jnp.where(w > 0.0, s, NEG_INF)
    m = jnp.maximum(jnp.max(s, axis=1, keepdims=True), s_new)
    p = jnp.exp2(s - m) * w
    p_new = float(len(DILATIONS)) * jnp.exp2(s_new - m)
    den = jnp.sum(p, axis=1, keepdims=True) + p_new
    folded = []
    for h in range(A_HEADS):
        acc = None
        for t in range(buf_a // LANES):
            cols = slice(t * LANES, (t + 1) * LANES)
            part = ca_ref[1, h * HEAD_DIM:(h + 1) * HEAD_DIM, cols] * p[h:h + 1, cols]
            acc = part if acc is None else acc + part
        folded.append(acc)
    o = jnp.sum(jnp.concatenate(folded, axis=0).T, axis=0, keepdims=True)
    oa_ref[...] = (o + spread(p_new) * va_new) / spread(den)

    prow = lax.broadcasted_iota(jnp.int32, (B_HEADS, LANES), 0)
    pcol = lax.broadcasted_iota(jnp.int32, (B_HEADS, LANES), 1) // HEAD_DIM
    qb_rows = jnp.zeros((B_HEADS, LANES), jnp.float32)
    for t in range(B_W // LANES):
        qb_rows = jnp.where(prow // 2 == t,
                            jnp.broadcast_to(qb[:, t * LANES:(t + 1) * LANES], (B_HEADS, LANES)), qb_rows)
    own_b = (prow % 2) == pcol
    qb_bd = jnp.where(own_b, qb_rows, 0.0)
    sb = jnp.dot(qb_bd.astype(jnp.bfloat16), cb_ref[0].astype(jnp.bfloat16),
                 preferred_element_type=jnp.float32)
    sb_new = jnp.sum(qb_bd * kb_new, axis=1, keepdims=True)
    sink = sink_ref[...]
    mb = jnp.maximum(jnp.maximum(jnp.max(sb, axis=1, keepdims=True), sb_new), sink)
    pb = jnp.exp2(sb - mb)
    pb_new = jnp.exp2(sb_new - mb)
    den_b = jnp.sum(pb, axis=1, keepdims=True) + pb_new + jnp.exp2(sink - mb)
    ob = lax.dot_general(pb.astype(jnp.bfloat16), cb_ref[1].astype(jnp.bfloat16),
                         (((1,), (1,)), ((), ())), preferred_element_type=jnp.float32)
    ob = (ob + pb_new * vb_new) / den_b
    lo = _lane_lo((1, LANES))
    for t in range(B_W // LANES):
        ob_ref[:, t * LANES:(t + 1) * LANES] = jnp.where(lo, ob[2 * t:2 * t + 1], ob[2 * t + 1:2 * t + 2])


def _sample_weights(buf):
    dist = buf - np.arange(buf)
    w = np.zeros((buf,), np.float32)
    for d in DILATIONS:
        w += ((dist % d == 0) & (dist <= BAND * d)).astype(np.float32)
    return jnp.asarray(w[None, :])


def _attn_kernel(*refs, phases, seq, with_sink, host_phases):
    refs = list(refs)
    q_ref, k_ref, v_ref, bias_ref = refs[:4]
    del refs[:4]
    sink_ref = refs.pop(0) if with_sink else None
    sample_in = [refs.pop(0) for _ in range(5)] if host_phases else None
    o_ref = refs.pop(0)
    sample_out = [refs.pop(0) for _ in range(2)] if host_phases else None
    dils = [d for phase in phases for d, _, _, _ in phase]
    multi = len(set(dils)) > 1
    if multi:
        acc_ref, m_ref, l_ref = refs
    lo = _lane_lo((BAND, LANES))

    def pair(a, b):
        return jnp.where(lo, a, b)

    head_lanes = (jnp.where(lo, 1.0, 0.0).astype(jnp.bfloat16),
                  jnp.where(lo, 0.0, 1.0).astype(jnp.bfloat16))

    def by_head(a):
        return a * head_lanes[0], a * head_lanes[1]

    ones = tuple(jnp.concatenate([t, t], axis=0) for t in head_lanes)

    def run(d, r, i_lo, i_hi):
        first = d == dils[0]
        last = d == dils[-1]

        def rows(start):
            return pl.ds(start, BAND) if d == 1 else pl.ds(start, BAND, stride=d)

        def block(i, carry):
            kp, vp = carry
            start = r + i * (BAND * d)
            q = q_ref[rows(start), :]
            kc = k_ref[rows(start), :].astype(jnp.bfloat16)
            vc = by_head(v_ref[rows(start), :].astype(jnp.bfloat16))
            if q.dtype == jnp.bfloat16:
                q2 = jnp.concatenate(by_head(q), axis=0)
            else:
                q2 = jnp.concatenate([jnp.where(lo, q, 0.0), jnp.where(lo, 0.0, q)],
                                     axis=0).astype(jnp.bfloat16)
            kk = jnp.concatenate([kp, kc], axis=0)
            s = lax.dot_general(q2, kk, (((1,), (1,)), ((), ())),
                                preferred_element_type=jnp.float32)
            s = s + bias_ref[min(i, 1)]
            m = jnp.max(s, axis=1, keepdims=True)
            p = jnp.exp2(s - m).astype(jnp.bfloat16)
            vv = jnp.concatenate(
                [jnp.concatenate([jnp.concatenate([vp[h], vc[h]], axis=0), ones[h]], axis=1)
                 for h in range(2)], axis=0)
            pv = jnp.dot(jnp.concatenate([p[:BAND], p[BAND:]], axis=1), vv,
                         preferred_element_type=jnp.float32)
            o_n = pv[:, :LANES]
            l_n = pv[:, LANES:]
            m_n = pair(jnp.broadcast_to(m[:BAND], (BAND, LANES)),
                       jnp.broadcast_to(m[BAND:], (BAND, LANES)))
            if multi and not first:
                m_o = m_ref[rows(start), :]
                m_t = jnp.maximum(m_o, m_n)
                a_o = jnp.exp2(m_o - m_t)
                a_n = jnp.exp2(m_n - m_t)
                o_n = acc_ref[rows(start), :] * a_o + o_n * a_n
                l_n = l_ref[rows(start), :] * a_o + l_n * a_n
                m_n = m_t
            if last:
                if with_sink:
                    sink = sink_ref[...]
                    m_t = jnp.maximum(m_n, sink)
                    a_n = jnp.exp2(m_n - m_t)
                    o_n = o_n * a_n
                    l_n = l_n * a_n + jnp.exp2(sink - m_t)
                o_ref[rows(start), :] = (o_n / l_n).astype(o_ref.dtype)
            else:
                acc_ref[rows(start), :] = o_n
                l_ref[rows(start), :] = l_n
                m_ref[rows(start), :] = m_n
            return kc, vc

        if i_lo == 0:
            zero = jnp.zeros((BAND, LANES), jnp.bfloat16)
            carry = (zero, (zero, zero))
        else:
            start = r + (i_lo - 1) * (BAND * d)
            carry = (k_ref[rows(start), :].astype(jnp.bfloat16),
                     by_head(v_ref[rows(start), :].astype(jnp.bfloat16)))
        for i in range(i_lo, i_hi):
            carry = block(i, carry)

    def run_phase(ph):
        if ph in host_phases:
            _sample_step(*sample_in, *sample_out)
        for seg in phases[ph]:
            run(*seg)

    if len(phases) == 1:
        run_phase(0)
    else:
        for ph in range(len(phases)):
            pl.when(pl.program_id(2) == ph)(functools.partial(run_phase, ph))


def _band_bias():
    qi = np.arange(BAND)[:, None]
    ki = np.arange(2 * BAND)[None, :]
    dist = BAND + qi - ki
    band = (dist >= 0) & (dist <= BAND)
    full = np.where(band, 0.0, NEG_INF).astype(np.float32)
    head = np.where(band & (ki >= BAND), 0.0, NEG_INF).astype(np.float32)
    both = np.stack([head, full])
    return jnp.asarray(np.concatenate([both, both], axis=1))


def _phase_plan(seq, dilations, shares):
    runs = [(d, r, lo, min(lo + RUN_BLOCKS, seq // d // BAND))
            for d in sorted(dilations, reverse=True) for r in range(d)
            for lo in range(0, seq // d // BAND, RUN_BLOCKS)]
    cost = lambda run: BLOCK_COST[run[0]] * (run[3] - run[2])
    bounds = np.cumsum(shares) * sum(cost(run) for run in runs)
    phases, cur, spent = [], [], 0.0
    for run in runs:
        cur.append(run)
        spent += cost(run)
        if len(phases) < len(shares) - 1 and spent >= bounds[len(phases)]:
            phases.append(tuple(cur))
            cur = []
    phases.append(tuple(cur))
    assert len(phases) == len(shares) and all(phases)
    return tuple(phases)


def _attention(hq3, q_tile, k_tile, v_tile, shared_kv, phases, sink=None, sample=None):
    b, seq, _ = hq3.shape
    n_pairs = A_W // LANES
    n_ph = len(phases)
    with_sink = sink is not None
    hosts = sample[5] if sample is not None else 0
    host_phases = tuple(range(n_ph - hosts, n_ph))
    kv_off = (lambda u: 0) if shared_kv else (lambda u: u)
    in_specs = [
        pl.BlockSpec((None, seq, LANES), lambda i, u, j: (i, 0, q_tile + u)),
        pl.BlockSpec((None, seq, LANES), lambda i, u, j: (i, 0, k_tile + kv_off(u))),
        pl.BlockSpec((None, seq, LANES), lambda i, u, j: (i, 0, v_tile + kv_off(u))),
        pl.BlockSpec((2, 2 * BAND, 2 * BAND), lambda i, u, j: (0, 0, 0)),
    ]
    args = [hq3, hq3, hq3, _band_bias()]
    if with_sink:
        in_specs.append(pl.BlockSpec((None, 1, LANES), lambda i, u, j: (u, 0, 0)))
        args.append(sink)
    out_specs = [pl.BlockSpec((None, seq, LANES), lambda i, u, j: (i, 0, u))]
    out_shape = [jax.ShapeDtypeStruct((b, seq, n_pairs * LANES), jnp.bfloat16)]
    if hosts:
        hq_s, cache_a_t, cache_b_t, sink_col, first, _ = sample
        n = b * n_pairs * hosts
        assert 0 < hosts <= n_ph and first + n <= hq_s.shape[0]
        buf_a = cache_a_t.shape[-1]
        buf_b = cache_b_t.shape[-1]

        def local(i, u, j):
            unit = i * n_pairs + u
            return jnp.maximum(unit * hosts + jnp.maximum(j - (n_ph - hosts), -1), 0)

        in_specs += [
            pl.BlockSpec((None, 1, HQ_W), lambda i, u, j: (first + local(i, u, j), 0, 0)),
            pl.BlockSpec((None, 2, A_W, buf_a), lambda i, u, j: (first + local(i, u, j), 0, 0, 0)),
            pl.BlockSpec((None, 2, B_KV_W, buf_b), lambda i, u, j: (first + local(i, u, j), 0, 0, 0)),
            pl.BlockSpec((1, buf_a), lambda i, u, j: (0, 0)),
            pl.BlockSpec((B_HEADS, 1), lambda i, u, j: (0, 0)),
        ]
        args += [hq_s.reshape(hq_s.shape[0], 1, HQ_W), cache_a_t, cache_b_t,
                 _sample_weights(buf_a), sink_col]
        out_specs += [pl.BlockSpec((None, 1, A_W), lambda i, u, j: (local(i, u, j), 0, 0)),
                      pl.BlockSpec((None, 1, B_W), lambda i, u, j: (local(i, u, j), 0, 0))]
        out_shape += [jax.ShapeDtypeStruct((n, 1, A_W), jnp.float32),
                      jax.ShapeDtypeStruct((n, 1, B_W), jnp.float32)]
    scratch = []
    if len({d for phase in phases for d, _, _, _ in phase}) > 1:
        scratch = [pltpu.VMEM((seq, LANES), jnp.float32)] * 3
    return pl.pallas_call(
        functools.partial(_attn_kernel, phases=phases, seq=seq, with_sink=with_sink,
                          host_phases=host_phases),
        grid=(b, n_pairs, n_ph),
        in_specs=in_specs,
        out_specs=out_specs,
        out_shape=out_shape,
        scratch_shapes=scratch,
        compiler_params=pltpu.CompilerParams(
            dimension_semantics=("arbitrary", "arbitrary", "arbitrary"),
            vmem_limit_bytes=VMEM_LIMIT),
        name="attn_b" if with_sink else "attn_a",
    )(*args)


def _out_kernel(x_ref, oa_ref, ob_ref, sg_ref, wo_ref, g_ref, b_ref, y_ref, *, alpha):
    sg = sg_ref[...].astype(jnp.float32)
    mix_a = (oa_ref[...].astype(jnp.float32) * sg[:, :A_W]).astype(jnp.bfloat16)
    mix_b = (ob_ref[...].astype(jnp.float32) * sg[:, A_W:]).astype(jnp.bfloat16)
    out = (jnp.dot(mix_a, wo_ref[:A_W, :], preferred_element_type=jnp.float32)
           + jnp.dot(mix_b, wo_ref[A_W:, :], preferred_element_type=jnp.float32))
    z = alpha * x_ref[...] + out
    mu = jnp.mean(z, axis=1, keepdims=True)
    zc = z - mu
    var = jnp.mean(zc * zc, axis=1, keepdims=True)
    y_ref[...] = zc * lax.rsqrt(var + LN_EPS) * g_ref[...] + b_ref[...]


def _output(x2, oa, ob, sg, wo, g, b, alpha, tm):
    n = x2.shape[0]
    row = lambda i: (i, 0)
    fixed = lambda i: (0, 0)
    return pl.pallas_call(
        functools.partial(_out_kernel, alpha=alpha),
        grid=(n // tm,),
        in_specs=[
            pl.BlockSpec((tm, D_MODEL), row),
            pl.BlockSpec((tm, A_W), row),
            pl.BlockSpec((tm, B_W), row),
            pl.BlockSpec((tm, G_W), row),
            pl.BlockSpec((G_W, D_MODEL), fixed),
            pl.BlockSpec((1, D_MODEL), fixed),
            pl.BlockSpec((1, D_MODEL), fixed),
        ],
        out_specs=pl.BlockSpec((tm, D_MODEL), row),
        out_shape=jax.ShapeDtypeStruct((n, D_MODEL), jnp.float32),
        compiler_params=pltpu.CompilerParams(
            dimension_semantics=("arbitrary",), vmem_limit_bytes=VMEM_LIMIT),
        name="out_proj",
    )(x2, oa, ob, sg, wo, g, b)


def _rope_tables(pos):
    inv = ROPE_THETA ** (-np.arange(0, ROT_DIM, 2, dtype=np.float64) / ROT_DIM)
    ang = pos.astype(np.float64)[:, None] * inv[None, :]
    cos = np.cos(ang)
    sin = np.sin(ang)
    n = pos.shape[0]
    half = ROT_DIM // 2
    pad = HEAD_DIM - ROT_DIM
    one = np.ones((n, pad))
    zero = np.zeros((n, pad))
    zh = np.zeros((n, half))
    cos_h = np.concatenate([cos, cos, one], axis=1)
    sinp_h = np.concatenate([zh, sin, zero], axis=1)
    sinm_h = np.concatenate([-sin, zh, zero], axis=1)
    rep = LANES // HEAD_DIM
    return tuple(jnp.asarray(np.tile(t, (1, rep)), jnp.float32) for t in (cos_h, sinp_h, sinm_h))


def _pair_b_heads(a, axis):
    group = B_HEADS // B_KV_HEADS
    shape = a.shape
    split = shape[:axis] + (B_KV_HEADS, group, shape[axis] // B_HEADS) + shape[axis + 1:]
    return jnp.swapaxes(a.reshape(split), axis, axis + 1).reshape(shape)


def _kv_rows(kvt, heads):
    n, _, _, rows = kvt.shape
    return jnp.transpose(kvt.reshape(n, 2, heads, HEAD_DIM, rows), (0, 4, 1, 2, 3))


def _layer(xp, xs, cache_a, cache_b, w_in, sinks, w_o, ln_g, ln_b, alpha, past_len):
    b, seq, _ = xp.shape
    nb = xs.shape[0]
    qa, ka, va, ga, qb, kb, vb, gb = jnp.split(
        w_in, np.cumsum([A_W, A_W, A_W, A_W, B_W, B_KV_W, B_KV_W])[:].tolist(), axis=1)
    q_scale = HEAD_DIM ** -0.5 * LOG2E
    w = jnp.concatenate([qa * q_scale, ka, va, _pair_b_heads(qb, 1) * q_scale, kb, vb,
                         ga, _pair_b_heads(gb, 1)], axis=1).astype(jnp.bfloat16)
    wo = jnp.concatenate([w_o[:A_W], _pair_b_heads(w_o[A_W:], 0)], axis=0).astype(jnp.bfloat16)
    sink_p = _pair_b_heads(sinks.astype(jnp.float32) * LOG2E, 0)
    sink_tiles = jnp.repeat(sink_p, HEAD_DIM).reshape(B_W // LANES, 1, LANES)
    g2 = ln_g.reshape(1, D_MODEL)
    b2 = ln_b.reshape(1, D_MODEL)

    tm = 512
    wa = min(A_WINDOW, seq)
    wb = min(B_WINDOW, seq)
    x2 = xp.reshape(b * seq, D_MODEL)
    hq, sg, kva_t, kvb_t, hb = _project(x2, w, *_rope_tables(np.arange(seq)), tm, seq, wa, wb)
    hq3 = hq.reshape(b, seq, HQ_W)

    xs2 = xs.reshape(nb, D_MODEL)
    pos_s = np.full((nb,), past_len, np.int64)
    hq_s, sg_s, new_a_t, new_b_t, _ = _project(xs2, w, *_rope_tables(pos_s), nb, nb, nb, nb)
    ca_t = jnp.transpose(cache_a, (0, 2, 3, 4, 1)).reshape(nb, 2, A_W, cache_a.shape[1])
    cb_t = jnp.transpose(cache_b, (0, 2, 3, 4, 1)).reshape(nb, 2, B_KV_W, cache_b.shape[1])

    sink_col = sink_p.reshape(B_HEADS, 1)
    units = b * (A_W // LANES)
    hosts_a = len(PHASE_SHARE) - 1
    assert nb == units * (hosts_a + 1), "every sample sequence has a hosting grid step"
    oa, oa_s1, ob_s1 = _attention(hq3, QA_T, KA_T, VA_T, False, _phase_plan(seq, DILATIONS, PHASE_SHARE),
                                  sample=(hq_s, ca_t, cb_t, sink_col, 0, hosts_a))
    ob, oa_s2, ob_s2 = _attention(hb.reshape(b, seq, HB_W), 0, KB_T - QB_T, VB_T - QB_T, True,
                                  _phase_plan(seq, (1,), (1.0,)),
                                  sink=sink_tiles, sample=(hq_s, ca_t, cb_t, sink_col, units * hosts_a, 1))
    oa_s = jnp.concatenate([oa_s1, oa_s2], axis=0)
    ob_s = jnp.concatenate([ob_s1, ob_s2], axis=0)
    y = _output(x2, oa.reshape(b * seq, A_W), ob.reshape(b * seq, B_W), sg, wo, g2, b2, alpha, 2 * tm)
    kv_a = _kv_rows(kva_t, A_HEADS)
    kv_b = _kv_rows(kvb_t, B_KV_HEADS)

    y_s = _output(xs2, oa_s.reshape(nb, A_W), ob_s.reshape(nb, B_W), sg_s, wo, g2, b2, alpha, nb)
    new_a = _kv_rows(new_a_t, A_HEADS).reshape(nb, 1, 2, A_HEADS, HEAD_DIM)
    new_b = _kv_rows(new_b_t, B_KV_HEADS).reshape(nb, 1, 2, B_KV_HEADS, HEAD_DIM)
    return (y.reshape(b, seq, D_MODEL), y_s.reshape(nb, 1, D_MODEL), kv_a, kv_b, new_a, new_b)


def kernel(x_prompt, x_sample, cache_a_kv, cache_b_kv, w_in, attn_sinks, w_o, ln_g, ln_b):
    depth = w_in.shape[0]
    assert depth == 1 and x_sample.shape[1] == 1, "single layer, one sample token per sequence"
    assert cache_a_kv.shape[2] == BAND * max(DILATIONS), "mixer-A cache covers every strided read"
    assert cache_b_kv.shape[2] == B_WINDOW
    alpha = (2 * depth) ** 0.25
    outs = _layer(x_prompt, x_sample, cache_a_kv[0], cache_b_kv[0], w_in[0], attn_sinks[0],
                  w_o[0], ln_g[0], ln_b[0], alpha, PAST_LEN)
    yp, ys, kv_a, kv_b, new_a, new_b = outs
    return (yp, ys, kv_a[None], kv_b[None], new_a[None], new_b[None])
```
